```python
import math
import jax, jax.numpy as jnp
from jax import lax
import numpy as np

D_MODEL = 1024
BATCH = 4
SEQ = 4096
DEPTH = 4

N_MIXERS = 2
N_GLA_LAYERS = (DEPTH + 1) // 2
N_ATTN_LAYERS = DEPTH // 2

GRID_W = 64

GLA_HEADS = 4
GLA_KEY_DIM = D_MODEL // 2
GLA_VAL_DIM = D_MODEL
GLA_DK = GLA_KEY_DIM // GLA_HEADS
GLA_DV = GLA_VAL_DIM // GLA_HEADS
GLA_GATE_RANK = 16
GLA_GATE_NORMALIZER = 16.0
GLA_CHUNK = 64
GLA_IN_DIM = 2 * GLA_KEY_DIM + 2 * GLA_VAL_DIM + 2 * GLA_GATE_RANK

ATTN_HEAD_DIM = 128
ATTN_Q_HEADS = D_MODEL // ATTN_HEAD_DIM
ATTN_KV_HEADS = 2
ATTN_GROUP = ATTN_Q_HEADS // ATTN_KV_HEADS
ATTN_QKV_DIM = (ATTN_Q_HEADS + 2 * ATTN_KV_HEADS) * ATTN_HEAD_DIM
QUERY_BLOCK = 128
ROPE_THETA = 10000.0
ROPE_PAIRS_PER_AXIS = ATTN_HEAD_DIM // 4

D_FF = 2816
CONV_WIDTH = 3

NORM_EPS = 1e-6

kernel_name = 'hybrid_gla_gqa2drope_convffn_encoder'


def rmsnorm(x, w):
    xf = x.astype(jnp.float32)
    y = xf * lax.rsqrt(jnp.mean(xf * xf, axis=-1, keepdims=True) + NORM_EPS)
    return (y * w.astype(jnp.float32)).astype(x.dtype)


def gla_chunked(q, k, v, log_a, strict):
    B, H, S, dk = q.shape
    dv = v.shape[-1]
    n = S // GLA_CHUNK
    c = lambda t: t.reshape(B, H, n, GLA_CHUNK, t.shape[-1])
    q, k, v, log_a = c(q), c(k), c(v), c(log_a)
    b = jnp.cumsum(log_a, axis=3)
    b_end = b[:, :, :, -1:, :]
    q_dec = q * jnp.exp(b)
    k_inv = k * jnp.exp(-b)
    mask = jnp.tril(jnp.ones((GLA_CHUNK, GLA_CHUNK), dtype=bool), k=-1 if strict else 0)
    att = jnp.where(mask, jnp.einsum('bhncd,bhnsd->bhncs', q_dec, k_inv), 0.0)
    o_intra = jnp.einsum('bhncs,bhnsv->bhncv', att, v)
    kv_chunk = jnp.einsum('bhncd,bhncv->bhndv', k * jnp.exp(b_end - b), v)
    decay_chunk = jnp.exp(b_end[:, :, :, 0, :])

    def step(state, inp):
        d, kv_c = inp
        return d[..., None] * state + kv_c, state

    _, s_prev = lax.scan(step, jnp.zeros((B, H, dk, dv), jnp.float32),
                         (jnp.moveaxis(decay_chunk, 2, 0), jnp.moveaxis(kv_chunk, 2, 0)))
    s_prev = jnp.moveaxis(s_prev, 0, 2)
    o_inter = jnp.einsum('bhncd,bhndv->bhncv', q_dec, s_prev)
    return (o_intra + o_inter).reshape(B, H, S, dv)


def gla_mixer(h, w_in, w_gate_up_f, b_gate_f, w_gate_up_b, b_gate_b, norm_w, w_out):
    B, S, _ = h.shape
    f32 = jnp.float32
    proj = h @ w_in
    q, k, v, g, r = jnp.split(
        proj, [GLA_KEY_DIM, 2 * GLA_KEY_DIM, 2 * GLA_KEY_DIM + GLA_VAL_DIM,
               2 * GLA_KEY_DIM + 2 * GLA_VAL_DIM], axis=-1)
    r_f, r_b = jnp.split(r, 2, axis=-1)

    def heads(t, d):
        return t.reshape(B, S, GLA_HEADS, d).transpose(0, 2, 1, 3).astype(f32)

    def log_decay(r_dir, w_up, b_up):
        logits = (r_dir @ w_up + b_up).astype(f32)
        return heads(jax.nn.log_sigmoid(logits) / GLA_GATE_NORMALIZER, GLA_DK)

    q = heads(q, GLA_DK) * (GLA_DK ** -0.5)
    k = heads(k, GLA_DK)
    v = heads(v, GLA_DV)
    la_f = log_decay(r_f, w_gate_up_f, b_gate_f)
    la_b = log_decay(r_b, w_gate_up_b, b_gate_b)
    flip = lambda t: jnp.flip(t, axis=2)
    o_f = gla_chunked(q, k, v, la_f, strict=False)
    o_b = flip(gla_chunked(flip(q), flip(k), flip(v), flip(la_b), strict=True))
    o = (o_f + o_b).transpose(0, 2, 1, 3)
    o = rmsnorm(o, norm_w) * jax.nn.silu(g.astype(f32).reshape(B, S, GLA_HEADS, GLA_DV))
    return o.reshape(B, S, GLA_VAL_DIM).astype(h.dtype) @ w_out


def apply_rope(x, cos, sin):
    half = x.shape[-1] // 2
    x1, x2 = x[..., :half], x[..., half:]
    return jnp.concatenate([x1 * cos - x2 * sin, x1 * sin + x2 * cos], axis=-1)


def attn_mixer(h, w_qkv, q_norm, k_norm, w_out, cos, sin):
    B, S, _ = h.shape
    proj = h @ w_qkv
    q, k, v = jnp.split(proj, [ATTN_Q_HEADS * ATTN_HEAD_DIM,
                               (ATTN_Q_HEADS + ATTN_KV_HEADS) * ATTN_HEAD_DIM], axis=-1)
    q = q.reshape(B, S, ATTN_Q_HEADS, ATTN_HEAD_DIM)
    k = k.reshape(B, S, ATTN_KV_HEADS, ATTN_HEAD_DIM)
    v = v.reshape(B, S, ATTN_KV_HEADS, ATTN_HEAD_DIM)
    q = apply_rope(rmsnorm(q, q_norm).astype(jnp.float32), cos, sin).astype(h.dtype)
    k = apply_rope(rmsnorm(k, k_norm).astype(jnp.float32), cos, sin).astype(h.dtype)
    n_blk = S // QUERY_BLOCK
    qb = q.reshape(B, S, ATTN_KV_HEADS, ATTN_GROUP, ATTN_HEAD_DIM).transpose(0, 2, 3, 1, 4)
    qb = qb.reshape(B, ATTN_KV_HEADS, ATTN_GROUP, n_blk, QUERY_BLOCK, ATTN_HEAD_DIM)
    qb = qb.transpose(3, 0, 1, 2, 4, 5)
    k = k.transpose(0, 2, 1, 3)
    v = v.transpose(0, 2, 1, 3)
    scale = ATTN_HEAD_DIM ** -0.5

    def block(q_blk):
        s = jnp.einsum('bkgqd,bksd->bkgqs', q_blk, k).astype(jnp.float32) * scale
        p = jax.nn.softmax(s, axis=-1).astype(v.dtype)
        return jnp.einsum('bkgqs,bksd->bkgqd', p, v)

    o = lax.map(block, qb)
    o = o.transpose(1, 0, 4, 2, 3, 5).reshape(B, S, ATTN_Q_HEADS * ATTN_HEAD_DIM)
    return o @ w_out


def conv_ffn(h, w_up, w_conv, b_conv, w_down):
    u = h @ w_up
    u = lax.conv_general_dilated(
        u, w_conv[:, None, :], window_strides=(1,), padding=[(1, 1)],
        dimension_numbers=('NWC', 'WIO', 'NWC'), feature_group_count=u.shape[-1]) + b_conv
    val, gate = jnp.split(u, 2, axis=-1)
    return (jax.nn.silu(gate) * val) @ w_down


def setup_inputs(seed: int = 0) -> dict:
    key = jax.random.key(seed)
    ks = jax.random.split(key, 20)
    nrm = lambda k, shape, s: jax.random.normal(k, shape, jnp.float32) * s
    NG, NA = N_GLA_LAYERS, N_ATTN_LAYERS
    return {
        'x': nrm(ks[0], (BATCH, SEQ, D_MODEL), 1.0),
        'norm_mix': 1.0 + nrm(ks[1], (DEPTH, D_MODEL), 0.01),
        'norm_ffn': 1.0 + nrm(ks[2], (DEPTH, D_MODEL), 0.01),
        'gla_w_in': nrm(ks[3], (NG, D_MODEL, GLA_IN_DIM), D_MODEL ** -0.5),
        'gla_w_gate_up_f': nrm(ks[4], (NG, GLA_GATE_RANK, GLA_KEY_DIM), GLA_GATE_RANK ** -0.5),
        'gla_b_gate_f': nrm(ks[5], (NG, GLA_KEY_DIM), 0.1),
        'gla_w_gate_up_b': nrm(ks[6], (NG, GLA_GATE_RANK, GLA_KEY_DIM), GLA_GATE_RANK ** -0.5),
        'gla_b_gate_b': nrm(ks[7], (NG, GLA_KEY_DIM), 0.1),
        'gla_norm': 1.0 + nrm(ks[8], (NG, GLA_DV), 0.01),
        'gla_w_out': nrm(ks[9], (NG, GLA_VAL_DIM, D_MODEL), GLA_VAL_DIM ** -0.5),
        'attn_w_qkv': nrm(ks[10], (NA, D_MODEL, ATTN_QKV_DIM), D_MODEL ** -0.5),
        'attn_q_norm': 1.0 + nrm(ks[11], (NA, ATTN_HEAD_DIM), 0.01),
        'attn_k_norm': 1.0 + nrm(ks[12], (NA, ATTN_HEAD_DIM), 0.01),
        'attn_w_out': nrm(ks[13], (NA, ATTN_Q_HEADS * ATTN_HEAD_DIM, D_MODEL),
                          (ATTN_Q_HEADS * ATTN_HEAD_DIM) ** -0.5),
        'ffn_w_up': nrm(ks[14], (DEPTH, D_MODEL, 2 * D_FF), D_MODEL ** -0.5),
        'ffn_w_conv': nrm(ks[15], (DEPTH, CONV_WIDTH, 2 * D_FF), CONV_WIDTH ** -0.5),
        'ffn_b_conv': nrm(ks[16], (DEPTH, 2 * D_FF), 0.01),
        'ffn_w_down': nrm(ks[17], (DEPTH, D_FF, D_MODEL), D_FF ** -0.5),
    }


def reference(x, norm_mix, norm_ffn, gla_w_in, gla_w_gate_up_f, gla_b_gate_f,
              gla_w_gate_up_b, gla_b_gate_b, gla_norm, gla_w_out,
              attn_w_qkv, attn_q_norm, attn_k_norm, attn_w_out,
              ffn_w_up, ffn_w_conv, ffn_b_conv, ffn_w_down):
    S = x.shape[1]
    rows = S // GRID_W
    f32 = jnp.float32
    row_idx = jnp.repeat(jnp.arange(rows, dtype=f32), GRID_W)
    col_idx = jnp.tile(jnp.arange(GRID_W, dtype=f32), rows)
    inv_freq = ROPE_THETA ** (-jnp.arange(ROPE_PAIRS_PER_AXIS, dtype=f32) / ROPE_PAIRS_PER_AXIS)
    ang = jnp.concatenate([row_idx[:, None] * inv_freq, col_idx[:, None] * inv_freq], axis=-1)
    cos = jnp.cos(ang)[None, :, None, :]
    sin = jnp.sin(ang)[None, :, None, :]

    for i in range(DEPTH):
        h = rmsnorm(x, norm_mix[i])
        j = i // N_MIXERS
        if i % N_MIXERS == 0:
            x = x + gla_mixer(h, gla_w_in[j], gla_w_gate_up_f[j], gla_b_gate_f[j],
                              gla_w_gate_up_b[j], gla_b_gate_b[j], gla_norm[j], gla_w_out[j])
        else:
            x = x + attn_mixer(h, attn_w_qkv[j], attn_q_norm[j], attn_k_norm[j],
                               attn_w_out[j], cos, sin)
        x = x + conv_ffn(rmsnorm(x, norm_ffn[i]), ffn_w_up[i], ffn_w_conv[i],
                         ffn_b_conv[i], ffn_w_down[i])
    return x
```

```python
import functools
import math

import jax
import jax.numpy as jnp
from jax import lax
from jax.experimental import pallas as pl
from jax.experimental.pallas import tpu as pltpu

F32 = jnp.float32
BF16 = jnp.bfloat16

D_MODEL = 1024
GRID_W = 64
NORM_EPS = 1e-6

GLA_HEADS = 4
GLA_KEY_DIM = 512
GLA_VAL_DIM = 1024
GLA_DK = 128
GLA_DV = 256
GLA_RANK = 16
GLA_GATE_NORMALIZER = 16.0
GLA_CHUNK = 64
GLA_MAIN_COLS = 2 * GLA_KEY_DIM + 2 * GLA_VAL_DIM
GLA_PROJ_COLS = GLA_MAIN_COLS + 128

HEAD_DIM = 128
Q_HEADS = 8
KV_HEADS = 2
GROUP = Q_HEADS // KV_HEADS
QKV_DIM = (Q_HEADS + 2 * KV_HEADS) * HEAD_DIM
ROPE_THETA = 10000.0
ROPE_PAIRS_PER_AXIS = HEAD_DIM // 4

D_FF = 2816
FF_CHUNK = 256
HALO = 16

VMEM_LIMIT = 56 * 1024 * 1024


def _cparams(sem):
    return pltpu.CompilerParams(dimension_semantics=sem, vmem_limit_bytes=VMEM_LIMIT)


def _rmsnorm_rows(x, w):
    ms = jnp.mean(x * x, axis=-1, keepdims=True)
    return x * lax.rsqrt(ms + NORM_EPS) * w


def _norm_proj_kernel(x_ref, nw_ref, w_ref, o_ref, *, col_chunks):
    h = _rmsnorm_rows(x_ref[...], nw_ref[...]).astype(BF16)
    for lo, hi in col_chunks:
        o_ref[:, lo:hi] = jnp.dot(h, w_ref[:, lo:hi], preferred_element_type=F32).astype(o_ref.dtype)


def _norm_proj(x, nw, w, tm):
    t, d = x.shape
    n = w.shape[1]
    chunks = []
    lo = 0
    while lo < n:
        hi = min(lo + 1024, n)
        chunks.append((lo, hi))
        lo = hi
    return pl.pallas_call(
        functools.partial(_norm_proj_kernel, col_chunks=tuple(chunks)),
        grid=(t // tm,),
        in_specs=[
            pl.BlockSpec((tm, d), lambda i: (i, 0)),
            pl.BlockSpec((1, d), lambda i: (0, 0)),
            pl.BlockSpec((d, n), lambda i: (0, 0)),
        ],
        out_specs=pl.BlockSpec((tm, n), lambda i: (i, 0)),
        out_shape=jax.ShapeDtypeStruct((t, n), BF16),
        compiler_params=_cparams(("parallel",)),
        name="gla_in_proj",
    )(x, nw, w)


def _qkv_proj_kernel(x_ref, nw_ref, w_ref, qn_ref, kn_ref, cos_ref, sin_ref, o_ref, *, q_scale):
    h = _rmsnorm_rows(x_ref[...], nw_ref[...]).astype(BF16)
    cosf = cos_ref[...]
    sinf = sin_ref[...]
    for c in range(QKV_DIM // 512):
        y = jnp.dot(h, w_ref[:, c * 512:(c + 1) * 512], preferred_element_type=F32)
        for j in range(4):
            head = c * 4 + j
            yh = y[:, j * HEAD_DIM:(j + 1) * HEAD_DIM]
            if head < Q_HEADS + KV_HEADS:
                gain = qn_ref[...] if head < Q_HEADS else kn_ref[...]
                yn = _rmsnorm_rows(yh, gain)
                yh = yn * cosf + pltpu.roll(yn, HEAD_DIM // 2, axis=1) * sinf
                if head < Q_HEADS:
                    yh = yh * q_scale
            o_ref[:, head * HEAD_DIM:(head + 1) * HEAD_DIM] = yh.astype(o_ref.dtype)


def _qkv_proj(x, nw, w, qn, kn, cosf, sinf, tm, seq, q_scale):
    t, d = x.shape
    tiles_per_seq = seq // tm
    return pl.pallas_call(
        functools.partial(_qkv_proj_kernel, q_scale=q_scale),
        grid=(t // tm,),
        in_specs=[
            pl.BlockSpec((tm, d), lambda i: (i, 0)),
            pl.BlockSpec((1, d), lambda i: (0, 0)),
            pl.BlockSpec((d, QKV_DIM), lambda i: (0, 0)),
            pl.BlockSpec((1, HEAD_DIM), lambda i: (0, 0)),
            pl.BlockSpec((1, HEAD_DIM), lambda i: (0, 0)),
            pl.BlockSpec((tm, HEAD_DIM), lambda i: (i % tiles_per_seq, 0)),
            pl.BlockSpec((tm, HEAD_DIM), lambda i: (i % tiles_per_seq, 0)),
        ],
        out_specs=pl.BlockSpec((tm, QKV_DIM), lambda i: (i, 0)),
        out_shape=jax.ShapeDtypeStruct((t, QKV_DIM), BF16),
        compiler_params=_cparams(("parallel",)),
        name="attn_qkv_proj",
    )(x, nw, w, qn, kn, cosf, sinf)


def _out_proj_kernel(x_ref, a_ref, w_ref, o_ref):
    o_ref[...] = x_ref[...] + jnp.dot(a_ref[...], w_ref[...], preferred_element_type=F32)


def _out_proj(x, a, w, tm):
    t, d = x.shape
    k = a.shape[1]
    return pl.pallas_call(
        _out_proj_kernel,
        grid=(t // tm,),
        in_specs=[
            pl.BlockSpec((tm, d), lambda i: (i, 0)),
            pl.BlockSpec((tm, k), lambda i: (i, 0)),
            pl.BlockSpec((k, d), lambda i: (0, 0)),
        ],
        out_specs=pl.BlockSpec((tm, d), lambda i: (i, 0)),
        out_shape=jax.ShapeDtypeStruct((t, d), F32),
        compiler_params=_cparams(("parallel",)),
        name="out_proj_residual",
    )(x, a, w)


def _log_sigmoid(x):
    return jnp.minimum(x, 0.0) - jnp.log(1.0 + jnp.exp(-jnp.abs(x)))


def _silu(x):
    return x / (1.0 + jnp.exp(-x))


def _gla_kernel(q_ref, k_ref, v_ref, g_ref, r_ref, wg_ref, bg_ref, nw_ref, o_ref,
                la_ref, oacc_ref, sf_ref, sb_ref, *, n_chunks):
    c_len = GLA_CHUNK
    logits = jnp.dot(r_ref[...], wg_ref[0], preferred_element_type=F32) + bg_ref[0]
    la_ref[...] = _log_sigmoid(logits) * (1.0 / GLA_GATE_NORMALIZER)
    sf_ref[...] = jnp.zeros_like(sf_ref)
    sb_ref[...] = jnp.zeros_like(sb_ref)

    row = lax.broadcasted_iota(jnp.int32, (c_len, c_len), 0)
    col = lax.broadcasted_iota(jnp.int32, (c_len, c_len), 1)
    lower_incl = row >= col
    upper_strict = col > row
    cum_fwd = jnp.where(lower_incl, 1.0, 0.0).astype(BF16)
    cum_bwd = jnp.where(col >= row, 1.0, 0.0).astype(BF16)
    q_scale = GLA_DK ** -0.5
    nw = nw_ref[...]

    def one_direction(c, forward):
        rows = pl.ds(pl.multiple_of(c * c_len, c_len), c_len)
        if forward:
            la = la_ref[rows, 0:GLA_DK]
            cum, mask, st_ref = cum_fwd, lower_incl, sf_ref
        else:
            la = la_ref[rows, GLA_DK:2 * GLA_DK]
            cum, mask, st_ref = cum_bwd, upper_strict, sb_ref
        la_hi = la.astype(BF16)
        la_lo = (la - la_hi.astype(F32)).astype(BF16)
        b = (jnp.dot(cum, la_hi, preferred_element_type=F32)
             + jnp.dot(cum, la_lo, preferred_element_type=F32))
        b_end = b[c_len - 1:c_len, :] if forward else b[0:1, :]
        q = q_ref[rows, :].astype(F32)
        k = k_ref[rows, :].astype(F32)
        v = v_ref[rows, :]
        q_dec = (q * (jnp.exp(b) * q_scale)).astype(BF16)
        k_inv = (k * jnp.exp(-b)).astype(BF16)
        k_out = (k * jnp.exp(b_end - b)).astype(BF16)
        att = lax.dot_general(q_dec, k_inv, (((1,), (1,)), ((), ())), preferred_element_type=F32)
        att = jnp.where(mask, att, 0.0).astype(BF16)
        state = st_ref[...]
        o = (jnp.dot(att, v, preferred_element_type=F32)
             + lax.dot_general(q_dec, state.astype(BF16), (((1,), (1,)), ((), ())),
                               preferred_element_type=F32))
        kv = lax.dot_general(v, k_out, (((0,), (0,)), ((), ())), preferred_element_type=F32)
        st_ref[...] = state * jnp.exp(b_end) + kv
        return rows, o

    def finalize(rows, o):
        g = g_ref[rows, :].astype(F32)
        o_ref[rows, :] = (_rmsnorm_rows(o, nw) * _silu(g)).astype(o_ref.dtype)

    half = n_chunks // 2

    def first_half(i, carry):
        rows_f, o_f = one_direction(i, True)
        oacc_ref[rows_f, :] = o_f
        rows_b, o_b = one_direction(n_chunks - 1 - i, False)
        oacc_ref[rows_b, :] = o_b
        return carry

    def second_half(i, carry):
        rows_f, o_f = one_direction(i, True)
        finalize(rows_f, o_f + oacc_ref[rows_f, :])
        rows_b, o_b = one_direction(n_chunks - 1 - i, False)
        finalize(rows_b, o_b + oacc_ref[rows_b, :])
        return carry

    lax.fori_loop(0, half, first_half, 0)
    lax.fori_loop(half, n_chunks, second_half, 0)


def _gla(proj, wg, bg, nw, batch, seq):
    t = batch * seq
    n_chunks = seq // GLA_CHUNK
    assert n_chunks % 2 == 0
    kq = GLA_KEY_DIM // GLA_DK
    return pl.pallas_call(
        functools.partial(_gla_kernel, n_chunks=n_chunks),
        grid=(batch, GLA_HEADS),
        in_specs=[
            pl.BlockSpec((seq, GLA_DK), lambda b, h: (b, h)),
            pl.BlockSpec((seq, GLA_DK), lambda b, h: (b, kq + h)),
            pl.BlockSpec((seq, GLA_DV), lambda b, h: (b, (2 * GLA_KEY_DIM) // GLA_DV + h)),
            pl.BlockSpec((seq, GLA_DV), lambda b, h: (b, (2 * GLA_KEY_DIM + GLA_VAL_DIM) // GLA_DV + h)),
            pl.BlockSpec((seq, 128), lambda b, h: (b, GLA_MAIN_COLS // 128)),
            pl.BlockSpec((1, 128, 2 * GLA_DK), lambda b, h: (h, 0, 0)),
            pl.BlockSpec((1, 1, 2 * GLA_DK), lambda b, h: (h, 0, 0)),
            pl.BlockSpec((1, GLA_DV), lambda b, h: (0, 0)),
        ],
        out_specs=pl.BlockSpec((seq, GLA_DV), lambda b, h: (b, h)),
        out_shape=jax.ShapeDtypeStruct((t, GLA_VAL_DIM), BF16),
        scratch_shapes=[
            pltpu.VMEM((seq, 2 * GLA_DK), F32),
            pltpu.VMEM((seq, GLA_DV), F32),
            pltpu.VMEM((GLA_DV, GLA_DK), F32),
            pltpu.VMEM((GLA_DV, GLA_DK), F32),
        ],
        compiler_params=_cparams(("parallel", "parallel")),
        name="gla_scan",
    )(proj, proj, proj, proj, proj, wg, bg, nw)


def _attn_kernel(q_ref, k_ref, v_ref, o_ref, qs_ref, *, tq, tk, n_kv):
    for g in range(GROUP):
        qs_ref[g * tq:(g + 1) * tq, :] = q_ref[:, g * HEAD_DIM:(g + 1) * HEAD_DIM]
    q = qs_ref[...]
    rows = GROUP * tq

    def body(j, carry):
        m, l, acc = carry
        ks = pl.ds(pl.multiple_of(j * tk, tk), tk)
        s = lax.dot_general(q, k_ref[ks, :], (((1,), (1,)), ((), ())), preferred_element_type=F32)
        m_new = jnp.maximum(m, jnp.max(s, axis=-1, keepdims=True))
        alpha = jnp.exp2(m - m_new)
        p = jnp.exp2(s - m_new)
        l = alpha * l + jnp.sum(p, axis=-1, keepdims=True)
        acc = alpha * acc + jnp.dot(p.astype(BF16), v_ref[ks, :], preferred_element_type=F32)
        return m_new, l, acc

    m0 = jnp.full((rows, 1), -jnp.inf, F32)
    l0 = jnp.zeros((rows, 1), F32)
    a0 = jnp.zeros((rows, HEAD_DIM), F32)
    m, l, acc = lax.fori_loop(0, n_kv, body, (m0, l0, a0))
    out = acc / l
    for g in range(GROUP):
        o_ref[:, g * HEAD_DIM:(g + 1) * HEAD_DIM] = out[g * tq:(g + 1) * tq, :].astype(o_ref.dtype)


def _attention(qkv, batch, seq, tq, tk):
    t = batch * seq
    n_q = seq // tq
    gw = GROUP * HEAD_DIM
    return pl.pallas_call(
        functools.partial(_attn_kernel, tq=tq, tk=tk, n_kv=seq // tk),
        grid=(batch, KV_HEADS, n_q),
        in_specs=[
            pl.BlockSpec((tq, gw), lambda b, h, i: (b * n_q + i, h)),
            pl.BlockSpec((seq, HEAD_DIM), lambda b, h, i: (b, Q_HEADS + h)),
            pl.BlockSpec((seq, HEAD_DIM), lambda b, h, i: (b, Q_HEADS + KV_HEADS + h)),
        ],
        out_specs=pl.BlockSpec((tq, gw), lambda b, h, i: (b * n_q + i, h)),
        out_shape=jax.ShapeDtypeStruct((t, Q_HEADS * HEAD_DIM), BF16),
        scratch_shapes=[pltpu.VMEM((GROUP * tq, HEAD_DIM), BF16)],
        compiler_params=_cparams(("parallel", "parallel", "arbitrary")),
        name="gqa_attention",
    )(qkv, qkv, qkv)


def _ffn_kernel(x_ref, xp_ref, xn_ref, nw_ref, wu_ref, wc_ref, bc_ref, wd_ref, o_ref,
                h_ref, u_ref, acc_ref, *, tm, tiles_per_seq):
    i = pl.program_id(0)
    nw = nw_ref[...]
    x = x_ref[...]
    keep_prev = jnp.where(i % tiles_per_seq == 0, 0.0, 1.0)
    keep_next = jnp.where(i % tiles_per_seq == tiles_per_seq - 1, 0.0, 1.0)
    h_ref[0:HALO, :] = (_rmsnorm_rows(xp_ref[...], nw) * keep_prev).astype(BF16)
    h_ref[HALO:HALO + tm, :] = _rmsnorm_rows(x, nw).astype(BF16)
    h_ref[HALO + tm:2 * HALO + tm, :] = (_rmsnorm_rows(xn_ref[...], nw) * keep_next).astype(BF16)
    acc_ref[...] = x
    h = h_ref[...]
    fc = FF_CHUNK
    for c in range(D_FF // fc):
        cols = slice(c * 2 * fc, (c + 1) * 2 * fc)
        u_ref[...] = jnp.dot(h, wu_ref[:, cols], preferred_element_type=F32)
        wc = wc_ref[:, cols]
        u = (u_ref[HALO - 1:HALO - 1 + tm, :] * wc[0:1, :]
             + u_ref[HALO:HALO + tm, :] * wc[1:2, :]
             + u_ref[HALO + 1:HALO + 1 + tm, :] * wc[2:3, :]
             + bc_ref[:, cols])
        act = (_silu(u[:, fc:]) * u[:, :fc]).astype(BF16)
        acc_ref[...] += jnp.dot(act, wd_ref[c * fc:(c + 1) * fc, :], preferred_element_type=F32)
    o_ref[...] = acc_ref[...]


def _ffn(x, nw, wu, wc, bc, wd, tm, seq):
    t, d = x.shape
    tiles_per_seq = seq // tm
    hb = tm // HALO
    n_hblk = t // HALO
    return pl.pallas_call(
        functools.partial(_ffn_kernel, tm=tm, tiles_per_seq=tiles_per_seq),
        grid=(t // tm,),
        in_specs=[
            pl.BlockSpec((tm, d), lambda i: (i, 0)),
            pl.BlockSpec((HALO, d), lambda i: (jnp.maximum(i * hb - 1, 0), 0)),
            pl.BlockSpec((HALO, d), lambda i: (jnp.minimum((i + 1) * hb, n_hblk - 1), 0)),
            pl.BlockSpec((1, d), lambda i: (0, 0)),
            pl.BlockSpec((d, 2 * D_FF), lambda i: (0, 0)),
            pl.BlockSpec((3, 2 * D_FF), lambda i: (0, 0)),
            pl.BlockSpec((1, 2 * D_FF), lambda i: (0, 0)),
            pl.BlockSpec((D_FF, d), lambda i: (0, 0)),
        ],
        out_specs=pl.BlockSpec((tm, d), lambda i: (i, 0)),
        out_shape=jax.ShapeDtypeStruct((t, d), F32),
        scratch_shapes=[
            pltpu.VMEM((tm + 2 * HALO, d), BF16),
            pltpu.VMEM((tm + 2 * HALO, 2 * FF_CHUNK), F32),
            pltpu.VMEM((tm, d), F32),
        ],
        compiler_params=_cparams(("parallel",)),
        name="conv_ffn",
    )(x, x, x, nw, wu, wc, bc, wd)


def _interleave_ff(a):
    lead = a.shape[:-1]
    n = D_FF // FF_CHUNK
    val = a[..., :D_FF].reshape(lead + (n, FF_CHUNK))
    gate = a[..., D_FF:].reshape(lead + (n, FF_CHUNK))
    return jnp.concatenate([val, gate], axis=-1).reshape(lead + (2 * D_FF,))


def _gla_gate_weights(w_f, b_f, w_b, b_b):
    wf = w_f.reshape(GLA_RANK, GLA_HEADS, GLA_DK).transpose(1, 0, 2)
    wb = w_b.reshape(GLA_RANK, GLA_HEADS, GLA_DK).transpose(1, 0, 2)
    z = jnp.zeros_like(wf)
    top = jnp.concatenate([wf, z], axis=-1)
    mid = jnp.concatenate([z, wb], axis=-1)
    pad = jnp.zeros((GLA_HEADS, 128 - 2 * GLA_RANK, 2 * GLA_DK), w_f.dtype)
    wg = jnp.concatenate([top, mid, pad], axis=1).astype(BF16)
    bg = jnp.concatenate([b_f.reshape(GLA_HEADS, 1, GLA_DK), b_b.reshape(GLA_HEADS, 1, GLA_DK)], axis=-1)
    return wg, bg.astype(F32)


def _rope_tables(seq):
    rows = seq // GRID_W
    row_idx = jnp.repeat(jnp.arange(rows, dtype=F32), GRID_W)
    col_idx = jnp.tile(jnp.arange(GRID_W, dtype=F32), rows)
    inv_freq = ROPE_THETA ** (-jnp.arange(ROPE_PAIRS_PER_AXIS, dtype=F32) / ROPE_PAIRS_PER_AXIS)
    ang = jnp.concatenate([row_idx[:, None] * inv_freq, col_idx[:, None] * inv_freq], axis=-1)
    cos = jnp.cos(ang)
    sin = jnp.sin(ang)
    return jnp.concatenate([cos, cos], axis=-1), jnp.concatenate([-sin, sin], axis=-1)


def kernel(x, norm_mix, norm_ffn, gla_w_in, gla_w_gate_up_f, gla_b_gate_f, gla_w_gate_up_b, gla_b_gate_b,
           gla_norm, gla_w_out, attn_w_qkv, attn_q_norm, attn_k_norm, attn_w_out,
           ffn_w_up, ffn_w_conv, ffn_b_conv, ffn_w_down):
    batch, seq, d = x.shape
    depth = norm_mix.shape[0]
    t = batch * seq
    tm = 512
    cosf, sinf = _rope_tables(seq)
    q_scale = (HEAD_DIM ** -0.5) * math.log2(math.e)
    xf = x.reshape(t, d)
    for i in range(depth):
        j = i // 2
        nw = norm_mix[i].reshape(1, d)
        if i % 2 == 0:
            w_in = jnp.pad(gla_w_in[j], ((0, 0), (0, GLA_PROJ_COLS - gla_w_in.shape[-1]))).astype(BF16)
            wg, bg = _gla_gate_weights(gla_w_gate_up_f[j], gla_b_gate_f[j], gla_w_gate_up_b[j], gla_b_gate_b[j])
            proj = _norm_proj(xf, nw, w_in, tm)
            a = _gla(proj, wg, bg, gla_norm[j].reshape(1, GLA_DV), batch, seq)
            xf = _out_proj(xf, a, gla_w_out[j].astype(BF16), tm)
        else:
            qkv = _qkv_proj(xf, nw, attn_w_qkv[j].astype(BF16), attn_q_norm[j].reshape(1, HEAD_DIM),
                            attn_k_norm[j].reshape(1, HEAD_DIM), cosf, sinf, tm, seq, q_scale)
            a = _attention(qkv, batch, seq, tq=256, tk=512)
            xf = _out_proj(xf, a, attn_w_out[j].astype(BF16), tm)
        xf = _ffn(xf, norm_ffn[i].reshape(1, d), _interleave_ff(ffn_w_up[i]).astype(BF16),
                  _interleave_ff(ffn_w_conv[i]), _interleave_ff(ffn_b_conv[i]).reshape(1, 2 * D_FF),
                  ffn_w_down[i].astype(BF16), tm, seq)
    return xf.reshape(batch, seq, d)
```

```python
import functools
import math

import jax
import jax.numpy as jnp
from jax import lax
from jax.experimental import pallas as pl
from jax.experimental.pallas import tpu as pltpu

F32 = jnp.float32
BF16 = jnp.bfloat16

D_MODEL = 1024
GRID_W = 64
NORM_EPS = 1e-6

GLA_HEADS = 4
GLA_KEY_DIM = 512
GLA_VAL_DIM = 1024
GLA_DK = 128
GLA_DV = 256
GLA_RANK = 16
GLA_GATE_NORMALIZER = 16.0
GLA_CHUNK = 64
GLA_GROUP = 4
GLA_MAIN_COLS = 2 * GLA_KEY_DIM + 2 * GLA_VAL_DIM
GLA_PROJ_COLS = GLA_MAIN_COLS + 128

HEAD_DIM = 128
Q_HEADS = 8
KV_HEADS = 2
GROUP = Q_HEADS // KV_HEADS
QKV_DIM = (Q_HEADS + 2 * KV_HEADS) * HEAD_DIM
ROPE_THETA = 10000.0
ROPE_PAIRS_PER_AXIS = HEAD_DIM // 4

D_FF = 2816
FF_CHUNK = 256
FFN_ROW_BLOCK = 64
HALO = 16

VMEM_LIMIT = 56 * 1024 * 1024


def _cparams(sem):
    return pltpu.CompilerParams(dimension_semantics=sem, vmem_limit_bytes=VMEM_LIMIT)


def _rmsnorm_rows(x, w):
    ms = jnp.mean(x * x, axis=-1, keepdims=True)
    return x * lax.rsqrt(ms + NORM_EPS) * w


def _norm_proj_kernel(x_ref, nw_ref, w_ref, o_ref, *, col_chunks):
    h = _rmsnorm_rows(x_ref[...], nw_ref[...]).astype(BF16)
    for lo, hi in col_chunks:
        o_ref[:, lo:hi] = jnp.dot(h, w_ref[:, lo:hi], preferred_element_type=F32).astype(o_ref.dtype)


def _norm_proj(x, nw, w, tm):
    t, d = x.shape
    n = w.shape[1]
    chunks = []
    lo = 0
    while lo < n:
        hi = min(lo + 1024, n)
        chunks.append((lo, hi))
        lo = hi
    return pl.pallas_call(
        functools.partial(_norm_proj_kernel, col_chunks=tuple(chunks)),
        grid=(t // tm,),
        in_specs=[
            pl.BlockSpec((tm, d), lambda i: (i, 0)),
            pl.BlockSpec((1, d), lambda i: (0, 0)),
            pl.BlockSpec((d, n), lambda i: (0, 0)),
        ],
        out_specs=pl.BlockSpec((tm, n), lambda i: (i, 0)),
        out_shape=jax.ShapeDtypeStruct((t, n), BF16),
        compiler_params=_cparams(("parallel",)),
        name="gla_in_proj",
    )(x, nw, w)


def _qkv_proj_kernel(x_ref, nw_ref, w_ref, qn_ref, kn_ref, cos_ref, sin_ref, o_ref, *, q_scale):
    h = _rmsnorm_rows(x_ref[...], nw_ref[...]).astype(BF16)
    cosf = cos_ref[...]
    sinf = sin_ref[...]
    for c in range(QKV_DIM // 512):
        y = jnp.dot(h, w_ref[:, c * 512:(c + 1) * 512], preferred_element_type=F32)
        for j in range(4):
            head = c * 4 + j
            yh = y[:, j * HEAD_DIM:(j + 1) * HEAD_DIM]
            if head < Q_HEADS + KV_HEADS:
                gain = qn_ref[...] if head < Q_HEADS else kn_ref[...]
                yn = _rmsnorm_rows(yh, gain)
                yh = yn * cosf + pltpu.roll(yn, HEAD_DIM // 2, axis=1) * sinf
                if head < Q_HEADS:
                    yh = yh * q_scale
            o_ref[:, head * HEAD_DIM:(head + 1) * HEAD_DIM] = yh.astype(o_ref.dtype)


def _qkv_proj(x, nw, w, qn, kn, cosf, sinf, tm, seq, q_scale):
    t, d = x.shape
    tiles_per_seq = seq // tm
    return pl.pallas_call(
        functools.partial(_qkv_proj_kernel, q_scale=q_scale),
        grid=(t // tm,),
        in_specs=[
            pl.BlockSpec((tm, d), lambda i: (i, 0)),
            pl.BlockSpec((1, d), lambda i: (0, 0)),
            pl.BlockSpec((d, QKV_DIM), lambda i: (0, 0)),
            pl.BlockSpec((1, HEAD_DIM), lambda i: (0, 0)),
            pl.BlockSpec((1, HEAD_DIM), lambda i: (0, 0)),
            pl.BlockSpec((tm, HEAD_DIM), lambda i: (i % tiles_per_seq, 0)),
            pl.BlockSpec((tm, HEAD_DIM), lambda i: (i % tiles_per_seq, 0)),
        ],
        out_specs=pl.BlockSpec((tm, QKV_DIM), lambda i: (i, 0)),
        out_shape=jax.ShapeDtypeStruct((t, QKV_DIM), BF16),
        compiler_params=_cparams(("parallel",)),
        name="attn_qkv_proj",
    )(x, nw, w, qn, kn, cosf, sinf)


def _out_proj_kernel(x_ref, a_ref, w_ref, o_ref):
    o_ref[...] = x_ref[...] + jnp.dot(a_ref[...], w_ref[...], preferred_element_type=F32)


def _out_proj(x, a, w, tm):
    t, d = x.shape
    k = a.shape[1]
    return pl.pallas_call(
        _out_proj_kernel,
        grid=(t // tm,),
        in_specs=[
            pl.BlockSpec((tm, d), lambda i: (i, 0)),
            pl.BlockSpec((tm, k), lambda i: (i, 0)),
            pl.BlockSpec((k, d), lambda i: (0, 0)),
        ],
        out_specs=pl.BlockSpec((tm, d), lambda i: (i, 0)),
        out_shape=jax.ShapeDtypeStruct((t, d), F32),
        compiler_params=_cparams(("parallel",)),
        name="out_proj_residual",
    )(x, a, w)


def _log_sigmoid(x):
    return jnp.minimum(x, 0.0) - jnp.log(1.0 + jnp.exp(-jnp.abs(x)))


def _silu(x):
    return x / (1.0 + jnp.exp(-x))


def _gla_kernel(q_ref, k_ref, v_ref, g_ref, r_ref, wg_ref, bg_ref, nw_ref, o_ref,
                la_ref, oacc_ref, sf_ref, sb_ref, *, n_chunks):
    c_len = GLA_CHUNK
    n_sub = GLA_GROUP
    g_len = n_sub * c_len
    n_groups = n_chunks // n_sub
    logits = jnp.dot(r_ref[...], wg_ref[0], preferred_element_type=F32) + bg_ref[0]
    la_ref[...] = _log_sigmoid(logits) * (1.0 / GLA_GATE_NORMALIZER)
    sf_ref[...] = jnp.zeros_like(sf_ref)
    sb_ref[...] = jnp.zeros_like(sb_ref)

    row = lax.broadcasted_iota(jnp.int32, (g_len, g_len), 0)
    col = lax.broadcasted_iota(jnp.int32, (g_len, g_len), 1)
    same_chunk = (row // c_len) == (col // c_len)
    q_scale = GLA_DK ** -0.5
    nw = nw_ref[...]
    trans_b = (((1,), (1,)), ((), ()))
    trans_a = (((0,), (0,)), ((), ()))

    def group(i, final):
        dirs = []
        for forward in (True, False):
            if forward:
                base = pl.multiple_of(i * g_len, g_len)
                lanes = slice(0, GLA_DK)
                tri = row >= col
                mask = same_chunk & tri
                st_ref = sf_ref
            else:
                base = pl.multiple_of((n_groups - 1 - i) * g_len, g_len)
                lanes = slice(GLA_DK, 2 * GLA_DK)
                mask = same_chunk & (col > row)
                tri = col >= row
                st_ref = sb_ref
            cum = jnp.where(same_chunk & tri, 1.0, 0.0).astype(BF16)
            dirs.append(dict(forward=forward, base=base, lanes=lanes, mask=mask, cum=cum, st_ref=st_ref,
                             rows=pl.ds(base, g_len)))

        for d in dirs:
            la = la_ref[d["rows"], d["lanes"]]
            la_hi = la.astype(BF16)
            la_lo = (la - la_hi.astype(F32)).astype(BF16)
            d["b"] = (jnp.dot(d["cum"], la_hi, preferred_element_type=F32)
                      + jnp.dot(d["cum"], la_lo, preferred_element_type=F32))

        for d in dirs:
            b = d["b"]
            ends = []
            for u in range(n_sub):
                r = u * c_len + (c_len - 1 if d["forward"] else 0)
                ends.append(b[r:r + 1, :])
            b_end = jnp.concatenate([jnp.broadcast_to(e, (c_len, GLA_DK)) for e in ends], axis=0)
            q = q_ref[d["rows"], :].astype(F32)
            k = k_ref[d["rows"], :].astype(F32)
            d["v"] = v_ref[d["rows"], :]
            d["q_dec"] = (q * (jnp.exp(b) * q_scale)).astype(BF16)
            k_inv = (k * jnp.exp(-b)).astype(BF16)
            d["k_out"] = (k * jnp.exp(b_end - b)).astype(BF16)
            d["decay"] = [jnp.exp(e) for e in ends]
            att = lax.dot_general(d["q_dec"], k_inv, trans_b, preferred_element_type=F32)
            d["att"] = jnp.where(d["mask"], att, 0.0).astype(BF16)

        for d in dirs:
            d["o"] = jnp.dot(d["att"], d["v"], preferred_element_type=F32)
            d["kv"] = []
            for u in range(n_sub):
                sl = slice(u * c_len, (u + 1) * c_len)
                d["kv"].append(lax.dot_general(d["k_out"][sl, :], d["v"][sl, :], trans_a,
                                               preferred_element_type=F32))

        for d in dirs:
            state = d["st_ref"][...]
            order = range(n_sub) if d["forward"] else range(n_sub - 1, -1, -1)
            for u in order:
                sl = slice(u * c_len, (u + 1) * c_len)
                o_u = d["o"][sl, :] + jnp.dot(d["q_dec"][sl, :], state.astype(BF16),
                                              preferred_element_type=F32)
                dec_col = jnp.broadcast_to(d["decay"][u], (GLA_DK, GLA_DK)).T
                state = state * jnp.concatenate([dec_col] * (GLA_DV // GLA_DK), axis=1) + d["kv"][u]
                rows_u = pl.ds(pl.multiple_of(d["base"] + u * c_len, c_len), c_len)
                if final:
                    o_sum = o_u + oacc_ref[rows_u, :]
                    g = g_ref[rows_u, :].astype(F32)
                    o_ref[rows_u, :] = (_rmsnorm_rows(o_sum, nw) * _silu(g)).astype(o_ref.dtype)
                else:
                    oacc_ref[rows_u, :] = o_u
            d["st_ref"][...] = state

    half = n_groups // 2

    def first_half(i, carry):
        group(i, False)
        return carry

    def second_half(i, carry):
        group(i, True)
        return carry

    lax.fori_loop(0, half, first_half, 0)
    lax.fori_loop(half, n_groups, second_half, 0)


def _gla(proj, wg, bg, nw, batch, seq):
    t = batch * seq
    n_chunks = seq // GLA_CHUNK
    assert n_chunks % (2 * GLA_GROUP) == 0
    kq = GLA_KEY_DIM // GLA_DK
    return pl.pallas_call(
        functools.partial(_gla_kernel, n_chunks=n_chunks),
        grid=(batch, GLA_HEADS),
        in_specs=[
            pl.BlockSpec((seq, GLA_DK), lambda b, h: (b, h)),
            pl.BlockSpec((seq, GLA_DK), lambda b, h: (b, kq + h)),
            pl.BlockSpec((seq, GLA_DV), lambda b, h: (b, (2 * GLA_KEY_DIM) // GLA_DV + h)),
            pl.BlockSpec((seq, GLA_DV), lambda b, h: (b, (2 * GLA_KEY_DIM + GLA_VAL_DIM) // GLA_DV + h)),
            pl.BlockSpec((seq, 128), lambda b, h: (b, GLA_MAIN_COLS // 128)),
            pl.BlockSpec((1, 128, 2 * GLA_DK), lambda b, h: (h, 0, 0)),
            pl.BlockSpec((1, 1, 2 * GLA_DK), lambda b, h: (h, 0, 0)),
            pl.BlockSpec((1, GLA_DV), lambda b, h: (0, 0)),
        ],
        out_specs=pl.BlockSpec((seq, GLA_DV), lambda b, h: (b, h)),
        out_shape=jax.ShapeDtypeStruct((t, GLA_VAL_DIM), BF16),
        scratch_shapes=[
            pltpu.VMEM((seq, 2 * GLA_DK), F32),
            pltpu.VMEM((seq, GLA_DV), F32),
            pltpu.VMEM((GLA_DK, GLA_DV), F32),
            pltpu.VMEM((GLA_DK, GLA_DV), F32),
        ],
        compiler_params=_cparams(("parallel", "parallel")),
        name="gla_scan",
    )(proj, proj, proj, proj, proj, wg, bg, nw)


def _attn_kernel(q_ref, k_ref, v_ref, o_ref, s_ref, m_ref, acc_ref, va_ref, *, tq, tk, n_q, n_kv):
    assert n_kv % 2 == 0

    def scores(tile, blk, slot):
        rows = pl.ds(pl.multiple_of(tile * tq, tq), tq)
        kb = k_ref[blk * tk:(blk + 1) * tk, :]
        for g in range(GROUP):
            qg = q_ref[rows, g * HEAD_DIM:(g + 1) * HEAD_DIM]
            s_ref[slot, g * tq:(g + 1) * tq, :] = lax.dot_general(
                qg, kb, (((1,), (1,)), ((), ())), preferred_element_type=F32)

    def softmax_pv(tile, blk, slot):
        vb = va_ref[blk * tk:(blk + 1) * tk, :]
        for g in range(GROUP):
            gr = slice(g * tq, (g + 1) * tq)
            s = s_ref[slot, gr, :]
            m_blk = jnp.max(s, axis=-1, keepdims=True)
            if blk == 0:
                m_new = m_blk
            else:
                m_old = m_ref[gr, :]
                m_new = jnp.maximum(m_old, m_blk)
                alpha = jnp.exp2(m_old - m_new)
            p = jnp.exp2(s - m_new).astype(BF16)
            pv = jnp.dot(p, vb, preferred_element_type=F32)
            acc_new = pv if blk == 0 else alpha * acc_ref[gr, :] + pv
            if blk == n_kv - 1:
                rows = pl.ds(pl.multiple_of(tile * tq, tq), tq)
                o_ref[rows, g * HEAD_DIM:(g + 1) * HEAD_DIM] = (
                    acc_new[:, :HEAD_DIM] / acc_new[:, HEAD_DIM:]).astype(o_ref.dtype)
            else:
                m_ref[gr, :] = m_new
                acc_ref[gr, :] = acc_new

    va_ref[:, :HEAD_DIM] = v_ref[...]
    va_ref[:, HEAD_DIM:] = jnp.ones((va_ref.shape[0], HEAD_DIM), BF16)
    scores(0, 0, 0)

    def tile_body(qi, carry):
        for j in range(n_kv):
            if j + 1 < n_kv:
                scores(qi, j + 1, (j + 1) % 2)
            else:
                scores(jnp.minimum(qi + 1, n_q - 1), 0, 0)
            softmax_pv(qi, j, j % 2)
        return carry

    lax.fori_loop(0, n_q, tile_body, 0)


def _attention(qkv, batch, seq, tq, tk):
    t = batch * seq
    gw = GROUP * HEAD_DIM
    rows = GROUP * tq
    return pl.pallas_call(
        functools.partial(_attn_kernel, tq=tq, tk=tk, n_q=seq // tq, n_kv=seq // tk),
        grid=(batch, KV_HEADS),
        in_specs=[
            pl.BlockSpec((seq, gw), lambda b, h: (b, h)),
            pl.BlockSpec((seq, HEAD_DIM), lambda b, h: (b, Q_HEADS + h)),
            pl.BlockSpec((seq, HEAD_DIM), lambda b, h: (b, Q_HEADS + KV_HEADS + h)),
        ],
        out_specs=pl.BlockSpec((seq, gw), lambda b, h: (b, h)),
        out_shape=jax.ShapeDtypeStruct((t, Q_HEADS * HEAD_DIM), BF16),
        scratch_shapes=[
            pltpu.VMEM((2, rows, tk), F32),
            pltpu.VMEM((rows, 1), F32),
            pltpu.VMEM((rows, 2 * HEAD_DIM), F32),
            pltpu.VMEM((seq, 2 * HEAD_DIM), BF16),
        ],
        compiler_params=_cparams(("parallel", "parallel")),
        name="gqa_attention",
    )(qkv, qkv, qkv)


def _ffn_kernel(x_ref, xp_ref, xn_ref, nw_ref, wu_ref, wc_ref, bc_ref, wd_ref, o_ref,
                h_ref, u_ref, act_ref, acc_ref, *, tm, tiles_per_seq):
    i = pl.program_id(0)
    nw = nw_ref[...]
    x = x_ref[...]
    keep_prev = jnp.where(i % tiles_per_seq == 0, 0.0, 1.0)
    keep_next = jnp.where(i % tiles_per_seq == tiles_per_seq - 1, 0.0, 1.0)
    h_ref[0:HALO, :] = (_rmsnorm_rows(xp_ref[...], nw) * keep_prev).astype(BF16)
    h_ref[HALO:HALO + tm, :] = _rmsnorm_rows(x, nw).astype(BF16)
    h_ref[HALO + tm:2 * HALO + tm, :] = (_rmsnorm_rows(xn_ref[...], nw) * keep_next).astype(BF16)
    acc_ref[...] = x
    h = h_ref[...]
    fc = FF_CHUNK
    n_chunks = D_FF // fc
    rb = FFN_ROW_BLOCK
    win = rb + 16

    def up_proj(c):
        u_ref[c % 2] = jnp.dot(h, wu_ref[:, c * 2 * fc:(c + 1) * 2 * fc], preferred_element_type=F32)

    up_proj(0)
    for c in range(n_chunks):
        if c + 1 < n_chunks:
            up_proj(c + 1)
        cols = slice(c * 2 * fc, (c + 1) * 2 * fc)
        wc = wc_ref[:, cols]
        bc = bc_ref[:, cols]
        for r in range(tm // rb):
            uw = u_ref[c % 2, HALO - 8 + r * rb:HALO + 8 + (r + 1) * rb, :]
            u_prev = pltpu.roll(uw, 1, axis=0)[8:8 + rb, :]
            u_next = pltpu.roll(uw, win - 1, axis=0)[8:8 + rb, :]
            u = u_prev * wc[0:1, :] + uw[8:8 + rb, :] * wc[1:2, :] + u_next * wc[2:3, :] + bc
            act_ref[r * rb:(r + 1) * rb, :] = (_silu(u[:, fc:]) * u[:, :fc]).astype(BF16)
        acc_ref[...] += jnp.dot(act_ref[...], wd_ref[c * fc:(c + 1) * fc, :], preferred_element_type=F32)
    o_ref[...] = acc_ref[...]


def _ffn(x, nw, wu, wc, bc, wd, tm, seq):
    t, d = x.shape
    tiles_per_seq = seq // tm
    hb = tm // HALO
    n_hblk = t // HALO
    return pl.pallas_call(
        functools.partial(_ffn_kernel, tm=tm, tiles_per_seq=tiles_per_seq),
        grid=(t // tm,),
        in_specs=[
            pl.BlockSpec((tm, d), lambda i: (i, 0)),
            pl.BlockSpec((HALO, d), lambda i: (jnp.maximum(i * hb - 1, 0), 0)),
            pl.BlockSpec((HALO, d), lambda i: (jnp.minimum((i + 1) * hb, n_hblk - 1), 0)),
            pl.BlockSpec((1, d), lambda i: (0, 0)),
            pl.BlockSpec((d, 2 * D_FF), lambda i: (0, 0)),
            pl.BlockSpec((3, 2 * D_FF), lambda i: (0, 0)),
            pl.BlockSpec((1, 2 * D_FF), lambda i: (0, 0)),
            pl.BlockSpec((D_FF, d), lambda i: (0, 0)),
        ],
        out_specs=pl.BlockSpec((tm, d), lambda i: (i, 0)),
        out_shape=jax.ShapeDtypeStruct((t, d), F32),
        scratch_shapes=[
            pltpu.VMEM((tm + 2 * HALO, d), BF16),
            pltpu.VMEM((2, tm + 2 * HALO, 2 * FF_CHUNK), F32),
            pltpu.VMEM((tm, FF_CHUNK), BF16),
            pltpu.VMEM((tm, d), F32),
        ],
        compiler_params=_cparams(("parallel",)),
        name="conv_ffn",
    )(x, x, x, nw, wu, wc, bc, wd)


def _interleave_ff(a):
    lead = a.shape[:-1]
    n = D_FF // FF_CHUNK
    val = a[..., :D_FF].reshape(lead + (n, FF_CHUNK))
    gate = a[..., D_FF:].reshape(lead + (n, FF_CHUNK))
    return jnp.concatenate([val, gate], axis=-1).reshape(lead + (2 * D_FF,))


def _gla_gate_weights(w_f, b_f, w_b, b_b):
    wf = w_f.reshape(GLA_RANK, GLA_HEADS, GLA_DK).transpose(1, 0, 2)
    wb = w_b.reshape(GLA_RANK, GLA_HEADS, GLA_DK).transpose(1, 0, 2)
    z = jnp.zeros_like(wf)
    top = jnp.concatenate([wf, z], axis=-1)
    mid = jnp.concatenate([z, wb], axis=-1)
    pad = jnp.zeros((GLA_HEADS, 128 - 2 * GLA_RANK, 2 * GLA_DK), w_f.dtype)
    wg = jnp.concatenate([top, mid, pad], axis=1).astype(BF16)
    bg = jnp.concatenate([b_f.reshape(GLA_HEADS, 1, GLA_DK), b_b.reshape(GLA_HEADS, 1, GLA_DK)], axis=-1)
    return wg, bg.astype(F32)


def _rope_tables(seq):
    rows = seq // GRID_W
    row_idx = jnp.repeat(jnp.arange(rows, dtype=F32), GRID_W)
    col_idx = jnp.tile(jnp.arange(GRID_W, dtype=F32), rows)
    inv_freq = ROPE_THETA ** (-jnp.arange(ROPE_PAIRS_PER_AXIS, dtype=F32) / ROPE_PAIRS_PER_AXIS)
    ang = jnp.concatenate([row_idx[:, None] * inv_freq, col_idx[:, None] * inv_freq], axis=-1)
    cos = jnp.cos(ang)
    sin = jnp.sin(ang)
    return jnp.concatenate([cos, cos], axis=-1), jnp.concatenate([-sin, sin], axis=-1)


def kernel(x, norm_mix, norm_ffn, gla_w_in, gla_w_gate_up_f, gla_b_gate_f, gla_w_gate_up_b, gla_b_gate_b,
           gla_norm, gla_w_out, attn_w_qkv, attn_q_norm, attn_k_norm, attn_w_out,
           ffn_w_up, ffn_w_conv, ffn_b_conv, ffn_w_down):
    batch, seq, d = x.shape
    depth = norm_mix.shape[0]
    t = batch * seq
    tm = 512
    cosf, sinf = _rope_tables(seq)
    q_scale = (HEAD_DIM ** -0.5) * math.log2(math.e)
    xf = x.reshape(t, d)
    for i in range(depth):
        j = i // 2
        nw = norm_mix[i].reshape(1, d)
        if i % 2 == 0:
            w_in = jnp.pad(gla_w_in[j], ((0, 0), (0, GLA_PROJ_COLS - gla_w_in.shape[-1]))).astype(BF16)
            wg, bg = _gla_gate_weights(gla_w_gate_up_f[j], gla_b_gate_f[j], gla_w_gate_up_b[j], gla_b_gate_b[j])
            proj = _norm_proj(xf, nw, w_in, tm)
            a = _gla(proj, wg, bg, gla_norm[j].reshape(1, GLA_DV), batch, seq)
            xf = _out_proj(xf, a, gla_w_out[j].astype(BF16), tm)
        else:
            qkv = _qkv_proj(xf, nw, attn_w_qkv[j].astype(BF16), attn_q_norm[j].reshape(1, HEAD_DIM),
                            attn_k_norm[j].reshape(1, HEAD_DIM), cosf, sinf, tm, seq, q_scale)
            a = _attention(qkv, batch, seq, tq=256, tk=min(1024, seq // 2))
            xf = _out_proj(xf, a, attn_w_out[j].astype(BF16), tm)
        xf = _ffn(xf, norm_ffn[i].reshape(1, d), _interleave_ff(ffn_w_up[i]).astype(BF16),
                  _interleave_ff(ffn_w_conv[i]), _interleave_ff(ffn_b_conv[i]).reshape(1, 2 * D_FF),
                  ffn_w_down[i].astype(BF16), tm, seq)
    return xf.reshape(batch, seq, d)
```

```python
import functools
import math

import jax
import jax.numpy as jnp
from jax import lax
from jax.experimental import pallas as pl
from jax.experimental.pallas import tpu as pltpu

F32 = jnp.float32
BF16 = jnp.bfloat16

D_MODEL = 1024
GRID_W = 64
NORM_EPS = 1e-6

GLA_HEADS = 4
GLA_KEY_DIM = 512
GLA_VAL_DIM = 1024
GLA_DK = 128
GLA_DV = 256
GLA_RANK = 16
GLA_GATE_NORMALIZER = 16.0
GLA_CHUNK = 64
GLA_GROUP = 4
GLA_GROUPS_PER_STEP = 4
GLA_MAIN_COLS = 2 * GLA_KEY_DIM + 2 * GLA_VAL_DIM
GLA_PROJ_COLS = GLA_MAIN_COLS + 128

HEAD_DIM = 128
Q_HEADS = 8
KV_HEADS = 2
GROUP = Q_HEADS // KV_HEADS
QKV_DIM = (Q_HEADS + 2 * KV_HEADS) * HEAD_DIM
ROPE_THETA = 10000.0
ROPE_PAIRS_PER_AXIS = HEAD_DIM // 4

D_FF = 2816
FF_CHUNK = 256
FFN_ROW_BLOCK = 64
HALO = 16

VMEM_LIMIT = 56 * 1024 * 1024


def _cparams(sem):
    return pltpu.CompilerParams(dimension_semantics=sem, vmem_limit_bytes=VMEM_LIMIT)


def _rmsnorm_rows(x, w):
    ms = jnp.mean(x * x, axis=-1, keepdims=True)
    return x * lax.rsqrt(ms + NORM_EPS) * w


def _norm_proj_kernel(x_ref, nw_ref, w_ref, o_ref, *, col_chunks):
    h = _rmsnorm_rows(x_ref[...], nw_ref[...]).astype(BF16)
    for lo, hi in col_chunks:
        o_ref[:, lo:hi] = jnp.dot(h, w_ref[:, lo:hi], preferred_element_type=F32).astype(o_ref.dtype)


def _norm_proj(x, nw, w, tm):
    t, d = x.shape
    n = w.shape[1]
    chunks = []
    lo = 0
    while lo < n:
        hi = min(lo + 1024, n)
        chunks.append((lo, hi))
        lo = hi
    return pl.pallas_call(
        functools.partial(_norm_proj_kernel, col_chunks=tuple(chunks)),
        grid=(t // tm,),
        in_specs=[
            pl.BlockSpec((tm, d), lambda i: (i, 0)),
            pl.BlockSpec((1, d), lambda i: (0, 0)),
            pl.BlockSpec((d, n), lambda i: (0, 0)),
        ],
        out_specs=pl.BlockSpec((tm, n), lambda i: (i, 0)),
        out_shape=jax.ShapeDtypeStruct((t, n), BF16),
        compiler_params=_cparams(("parallel",)),
        name="gla_in_proj",
    )(x, nw, w)


def _qkv_proj_kernel(x_ref, nw_ref, w_ref, qn_ref, kn_ref, cos_ref, sin_ref, o_ref, *, q_scale):
    h = _rmsnorm_rows(x_ref[...], nw_ref[...]).astype(BF16)
    cosf = cos_ref[...]
    sinf = sin_ref[...]
    for c in range(QKV_DIM // 512):
        y = jnp.dot(h, w_ref[:, c * 512:(c + 1) * 512], preferred_element_type=F32)
        for j in range(4):
            head = c * 4 + j
            yh = y[:, j * HEAD_DIM:(j + 1) * HEAD_DIM]
            if head < Q_HEADS + KV_HEADS:
                gain = qn_ref[...] if head < Q_HEADS else kn_ref[...]
                yn = _rmsnorm_rows(yh, gain)
                yh = yn * cosf + pltpu.roll(yn, HEAD_DIM // 2, axis=1) * sinf
                if head < Q_HEADS:
                    yh = yh * q_scale
            o_ref[:, head * HEAD_DIM:(head + 1) * HEAD_DIM] = yh.astype(o_ref.dtype)


def _qkv_proj(x, nw, w, qn, kn, cosf, sinf, tm, seq, q_scale):
    t, d = x.shape
    tiles_per_seq = seq // tm
    return pl.pallas_call(
        functools.partial(_qkv_proj_kernel, q_scale=q_scale),
        grid=(t // tm,),
        in_specs=[
            pl.BlockSpec((tm, d), lambda i: (i, 0)),
            pl.BlockSpec((1, d), lambda i: (0, 0)),
            pl.BlockSpec((d, QKV_DIM), lambda i: (0, 0)),
            pl.BlockSpec((1, HEAD_DIM), lambda i: (0, 0)),
            pl.BlockSpec((1, HEAD_DIM), lambda i: (0, 0)),
            pl.BlockSpec((tm, HEAD_DIM), lambda i: (i % tiles_per_seq, 0)),
            pl.BlockSpec((tm, HEAD_DIM), lambda i: (i % tiles_per_seq, 0)),
        ],
        out_specs=pl.BlockSpec((tm, QKV_DIM), lambda i: (i, 0)),
        out_shape=jax.ShapeDtypeStruct((t, QKV_DIM), BF16),
        compiler_params=_cparams(("parallel",)),
        name="attn_qkv_proj",
    )(x, nw, w, qn, kn, cosf, sinf)


def _log_sigmoid(x):
    return jnp.minimum(x, 0.0) - jnp.log(1.0 + jnp.exp(-jnp.abs(x)))


def _silu(x):
    return x / (1.0 + jnp.exp(-x))


def _gla_kernel(q_ref, k_ref, v_ref, g_ref, r_ref, wg_ref, bg_ref, nw_ref, o_ref,
                la_ref, oacc_ref, sf_ref, sb_ref, *, n_chunks):
    c_len = GLA_CHUNK
    n_sub = GLA_GROUP
    g_len = n_sub * c_len
    n_groups = n_chunks // n_sub
    n_par = GLA_GROUPS_PER_STEP
    logits = jnp.dot(r_ref[...], wg_ref[0], preferred_element_type=F32) + bg_ref[0]
    la_ref[...] = _log_sigmoid(logits) * (1.0 / GLA_GATE_NORMALIZER)
    sf_ref[...] = jnp.zeros_like(sf_ref)
    sb_ref[...] = jnp.zeros_like(sb_ref)

    row = lax.broadcasted_iota(jnp.int32, (g_len, g_len), 0)
    col = lax.broadcasted_iota(jnp.int32, (g_len, g_len), 1)
    same_chunk = (row // c_len) == (col // c_len)
    cum_prefix = jnp.where(same_chunk & (row >= col), 1.0, 0.0).astype(BF16)
    q_scale = GLA_DK ** -0.5
    nw = nw_ref[...]
    trans_b = (((1,), (1,)), ((), ()))
    trans_a = (((0,), (0,)), ((), ()))

    def step(i, final):
        work = []
        for forward in (True, False):
            for s in range(n_par):
                gi = i * n_par + s
                g = gi if forward else n_groups - 1 - gi
                base = pl.multiple_of(g * g_len, g_len)
                if forward:
                    lanes, keep = slice(0, GLA_DK), row >= col
                else:
                    lanes, keep = slice(GLA_DK, 2 * GLA_DK), col > row
                work.append(dict(forward=forward, base=base, rows=pl.ds(base, g_len), lanes=lanes,
                                 mask=same_chunk & keep))

        n_half = len(work) // 2
        for fw, bw in zip(work[:n_half], work[n_half:]):
            la = jnp.concatenate([la_ref[fw["rows"], fw["lanes"]], la_ref[bw["rows"], bw["lanes"]]], axis=1)
            la_hi = la.astype(BF16)
            la_lo = (la - la_hi.astype(F32)).astype(BF16)
            pre = (jnp.dot(cum_prefix, la_hi, preferred_element_type=F32)
                   + jnp.dot(cum_prefix, la_lo, preferred_element_type=F32))
            fw["b"] = pre[:, :GLA_DK]
            pre_b = pre[:, GLA_DK:]
            tot = jnp.concatenate(
                [jnp.broadcast_to(pre_b[u * c_len + c_len - 1:u * c_len + c_len, :], (c_len, GLA_DK))
                 for u in range(n_sub)], axis=0)
            bw["b"] = tot - pre_b + la[:, GLA_DK:]

        for d in work:
            b = d["b"]
            ends = []
            for u in range(n_sub):
                r = u * c_len + (c_len - 1 if d["forward"] else 0)
                ends.append(b[r:r + 1, :])
            b_end = jnp.concatenate([jnp.broadcast_to(e, (c_len, GLA_DK)) for e in ends], axis=0)
            q = q_ref[d["rows"], :].astype(F32)
            k = k_ref[d["rows"], :].astype(F32)
            d["v"] = v_ref[d["rows"], :]
            d["q_dec"] = (q * (jnp.exp(b) * q_scale)).astype(BF16)
            k_inv = (k * jnp.exp(-b)).astype(BF16)
            d["k_out"] = (k * jnp.exp(b_end - b)).astype(BF16)
            d["decay"] = [jnp.exp(e) for e in ends]
            att = lax.dot_general(d["q_dec"], k_inv, trans_b, preferred_element_type=F32)
            d["att"] = jnp.where(d["mask"], att, 0.0).astype(BF16)

        for d in work:
            d["o"] = jnp.dot(d["att"], d["v"], preferred_element_type=F32)
            d["kv"] = []
            for u in range(n_sub):
                sl = slice(u * c_len, (u + 1) * c_len)
                d["kv"].append(lax.dot_general(d["k_out"][sl, :], d["v"][sl, :], trans_a,
                                               preferred_element_type=F32))

        for forward in (True, False):
            st_ref = sf_ref if forward else sb_ref
            state = st_ref[...]
            for d in (w for w in work if w["forward"] == forward):
                for u in (range(n_sub) if forward else range(n_sub - 1, -1, -1)):
                    sl = slice(u * c_len, (u + 1) * c_len)
                    o_u = d["o"][sl, :] + jnp.dot(d["q_dec"][sl, :], state.astype(BF16),
                                                  preferred_element_type=F32)
                    dec_col = jnp.broadcast_to(d["decay"][u], (GLA_DK, GLA_DK)).T
                    state = state * jnp.concatenate([dec_col] * (GLA_DV // GLA_DK), axis=1) + d["kv"][u]
                    rows_u = pl.ds(pl.multiple_of(d["base"] + u * c_len, c_len), c_len)
                    if final:
                        o_sum = o_u + oacc_ref[rows_u, :]
                        g = g_ref[rows_u, :].astype(F32)
                        o_ref[rows_u, :] = (_rmsnorm_rows(o_sum, nw) * _silu(g)).astype(o_ref.dtype)
                    else:
                        oacc_ref[rows_u, :] = o_u
            st_ref[...] = state

    n_steps = n_groups // n_par
    half = n_steps // 2

    def first_half(i, carry):
        step(i, False)
        return carry

    def second_half(i, carry):
        step(i, True)
        return carry

    lax.fori_loop(0, half, first_half, 0)
    lax.fori_loop(half, n_steps, second_half, 0)


def _gla(proj, wg, bg, nw, batch, seq):
    t = batch * seq
    n_chunks = seq // GLA_CHUNK
    assert n_chunks % (2 * GLA_GROUP * GLA_GROUPS_PER_STEP) == 0
    kq = GLA_KEY_DIM // GLA_DK
    return pl.pallas_call(
        functools.partial(_gla_kernel, n_chunks=n_chunks),
        grid=(batch, GLA_HEADS),
        in_specs=[
            pl.BlockSpec((seq, GLA_DK), lambda b, h: (b, h)),
            pl.BlockSpec((seq, GLA_DK), lambda b, h: (b, kq + h)),
            pl.BlockSpec((seq, GLA_DV), lambda b, h: (b, (2 * GLA_KEY_DIM) // GLA_DV + h)),
            pl.BlockSpec((seq, GLA_DV), lambda b, h: (b, (2 * GLA_KEY_DIM + GLA_VAL_DIM) // GLA_DV + h)),
            pl.BlockSpec((seq, 128), lambda b, h: (b, GLA_MAIN_COLS // 128)),
            pl.BlockSpec((1, 128, 2 * GLA_DK), lambda b, h: (h, 0, 0)),
            pl.BlockSpec((1, 1, 2 * GLA_DK), lambda b, h: (h, 0, 0)),
            pl.BlockSpec((1, GLA_DV), lambda b, h: (0, 0)),
        ],
        out_specs=pl.BlockSpec((seq, GLA_DV), lambda b, h: (b, h)),
        out_shape=jax.ShapeDtypeStruct((t, GLA_VAL_DIM), BF16),
        scratch_shapes=[
            pltpu.VMEM((seq, 2 * GLA_DK), F32),
            pltpu.VMEM((seq, GLA_DV), F32),
            pltpu.VMEM((GLA_DK, GLA_DV), F32),
            pltpu.VMEM((GLA_DK, GLA_DV), F32),
        ],
        compiler_params=_cparams(("parallel", "parallel")),
        name="gla_scan",
    )(proj, proj, proj, proj, proj, wg, bg, nw)


def _attn_kernel(q_ref, k_ref, v_ref, o_ref, s_ref, m_ref, acc_ref, va_ref, *, tq, tk, n_q, n_kv):
    assert n_kv % 2 == 0

    def scores(tile, blk, slot):
        rows = pl.ds(pl.multiple_of(tile * tq, tq), tq)
        kb = k_ref[blk * tk:(blk + 1) * tk, :]
        for g in range(GROUP):
            qg = q_ref[rows, g * HEAD_DIM:(g + 1) * HEAD_DIM]
            s_ref[slot, g * tq:(g + 1) * tq, :] = lax.dot_general(
                qg, kb, (((1,), (1,)), ((), ())), preferred_element_type=F32)

    def softmax_pv(tile, blk, slot):
        vb = va_ref[blk * tk:(blk + 1) * tk, :]
        for g in range(GROUP):
            gr = slice(g * tq, (g + 1) * tq)
            s = s_ref[slot, gr, :]
            m_blk = jnp.max(s, axis=-1, keepdims=True)
            if blk == 0:
                m_new = m_blk
            else:
                m_old = m_ref[gr, :]
                m_new = jnp.maximum(m_old, m_blk)
                alpha = jnp.exp2(m_old - m_new)
            p = jnp.exp2(s - m_new).astype(BF16)
            pv = jnp.dot(p, vb, preferred_element_type=F32)
            acc_new = pv if blk == 0 else alpha * acc_ref[gr, :] + pv
            if blk == n_kv - 1:
                rows = pl.ds(pl.multiple_of(tile * tq, tq), tq)
                o_ref[rows, g * HEAD_DIM:(g + 1) * HEAD_DIM] = (
                    acc_new[:, :HEAD_DIM] / acc_new[:, HEAD_DIM:]).astype(o_ref.dtype)
            else:
                m_ref[gr, :] = m_new
                acc_ref[gr, :] = acc_new

    va_ref[:, :HEAD_DIM] = v_ref[...]
    va_ref[:, HEAD_DIM:] = jnp.ones((va_ref.shape[0], HEAD_DIM), BF16)
    scores(0, 0, 0)

    def tile_body(qi, carry):
        for j in range(n_kv):
            if j + 1 < n_kv:
                scores(qi, j + 1, (j + 1) % 2)
            else:
                scores(jnp.minimum(qi + 1, n_q - 1), 0, 0)
            softmax_pv(qi, j, j % 2)
        return carry

    lax.fori_loop(0, n_q, tile_body, 0)


def _attention(qkv, batch, seq, tq, tk):
    t = batch * seq
    gw = GROUP * HEAD_DIM
    rows = GROUP * tq
    return pl.pallas_call(
        functools.partial(_attn_kernel, tq=tq, tk=tk, n_q=seq // tq, n_kv=seq // tk),
        grid=(batch, KV_HEADS),
        in_specs=[
            pl.BlockSpec((seq, gw), lambda b, h: (b, h)),
            pl.BlockSpec((seq, HEAD_DIM), lambda b, h: (b, Q_HEADS + h)),
            pl.BlockSpec((seq, HEAD_DIM), lambda b, h: (b, Q_HEADS + KV_HEADS + h)),
        ],
        out_specs=pl.BlockSpec((seq, gw), lambda b, h: (b, h)),
        out_shape=jax.ShapeDtypeStruct((t, Q_HEADS * HEAD_DIM), BF16),
        scratch_shapes=[
            pltpu.VMEM((2, rows, tk), F32),
            pltpu.VMEM((rows, 1), F32),
            pltpu.VMEM((rows, 2 * HEAD_DIM), F32),
            pltpu.VMEM((seq, 2 * HEAD_DIM), BF16),
        ],
        compiler_params=_cparams(("parallel", "parallel")),
        name="gqa_attention",
    )(qkv, qkv, qkv)


def _ffn_kernel(x_ref, xp_ref, xn_ref, a_ref, ap_ref, an_ref, wo_ref, nw_ref, wu_ref, wc_ref, bc_ref, wd_ref,
                o_ref, h_ref, u_ref, act_ref, acc_ref, *, tm, tiles_per_seq):
    i = pl.program_id(0)
    nw = nw_ref[...]
    wo = wo_ref[...]

    def mixed(xr, ar):
        return xr[...] + jnp.dot(ar[...], wo, preferred_element_type=F32)

    x = mixed(x_ref, a_ref)
    keep_prev = jnp.where(i % tiles_per_seq == 0, 0.0, 1.0)
    keep_next = jnp.where(i % tiles_per_seq == tiles_per_seq - 1, 0.0, 1.0)
    h_ref[0:HALO, :] = (_rmsnorm_rows(mixed(xp_ref, ap_ref), nw) * keep_prev).astype(BF16)
    h_ref[HALO:HALO + tm, :] = _rmsnorm_rows(x, nw).astype(BF16)
    h_ref[HALO + tm:2 * HALO + tm, :] = (_rmsnorm_rows(mixed(xn_ref, an_ref), nw) * keep_next).astype(BF16)
    acc_ref[...] = x
    h = h_ref[...]
    fc = FF_CHUNK
    n_chunks = D_FF // fc
    rb = FFN_ROW_BLOCK
    win = rb + 16

    def up_proj(c):
        u_ref[c % 2, 0] = jnp.dot(h, wu_ref[:, c * fc:(c + 1) * fc], preferred_element_type=F32)
        u_ref[c % 2, 1] = jnp.dot(h, wu_ref[:, D_FF + c * fc:D_FF + (c + 1) * fc], preferred_element_type=F32)

    def down_proj(c):
        acc_ref[...] += jnp.dot(act_ref[c % 2], wd_ref[c * fc:(c + 1) * fc, :], preferred_element_type=F32)

    def conv(slot, part, r, col0):
        uw = u_ref[slot, part, HALO - 8 + r * rb:HALO + 8 + (r + 1) * rb, :]
        wc = wc_ref[:, col0:col0 + fc]
        u_prev = pltpu.roll(uw, 1, axis=0)[8:8 + rb, :]
        u_next = pltpu.roll(uw, win - 1, axis=0)[8:8 + rb, :]
        return (u_prev * wc[0:1, :] + uw[8:8 + rb, :] * wc[1:2, :] + u_next * wc[2:3, :]
                + bc_ref[:, col0:col0 + fc])

    up_proj(0)
    for c in range(n_chunks):
        if c + 1 < n_chunks:
            up_proj(c + 1)
        if c > 0:
            down_proj(c - 1)
        for r in range(tm // rb):
            val = conv(c % 2, 0, r, c * fc)
            gate = conv(c % 2, 1, r, D_FF + c * fc)
            act_ref[c % 2, r * rb:(r + 1) * rb, :] = (_silu(gate) * val).astype(BF16)
    down_proj(n_chunks - 1)
    o_ref[...] = acc_ref[...]


def _mix_ffn(x, a, wo, nw, wu, wc, bc, wd, tm, seq):
    t, d = x.shape
    ka = a.shape[1]
    tiles_per_seq = seq // tm
    hb = tm // HALO
    n_hblk = t // HALO
    main = lambda i: (i, 0)
    prev = lambda i: (jnp.maximum(i * hb - 1, 0), 0)
    nxt = lambda i: (jnp.minimum((i + 1) * hb, n_hblk - 1), 0)
    const = lambda i: (0, 0)
    return pl.pallas_call(
        functools.partial(_ffn_kernel, tm=tm, tiles_per_seq=tiles_per_seq),
        grid=(t // tm,),
        in_specs=[
            pl.BlockSpec((tm, d), main),
            pl.BlockSpec((HALO, d), prev),
            pl.BlockSpec((HALO, d), nxt),
            pl.BlockSpec((tm, ka), main),
            pl.BlockSpec((HALO, ka), prev),
            pl.BlockSpec((HALO, ka), nxt),
            pl.BlockSpec((ka, d), const),
            pl.BlockSpec((1, d), const),
            pl.BlockSpec((d, 2 * D_FF), const),
            pl.BlockSpec((3, 2 * D_FF), const),
            pl.BlockSpec((1, 2 * D_FF), const),
            pl.BlockSpec((D_FF, d), const),
        ],
        out_specs=pl.BlockSpec((tm, d), main),
        out_shape=jax.ShapeDtypeStruct((t, d), F32),
        scratch_shapes=[
            pltpu.VMEM((tm + 2 * HALO, d), BF16),
            pltpu.VMEM((2, 2, tm + 2 * HALO, FF_CHUNK), F32),
            pltpu.VMEM((2, tm, FF_CHUNK), BF16),
            pltpu.VMEM((tm, d), F32),
        ],
        compiler_params=_cparams(("parallel",)),
        name="mix_conv_ffn",
    )(x, x, x, a, a, a, wo, nw, wu, wc, bc, wd)


def _gla_gate_weights(w_f, b_f, w_b, b_b):
    wf = w_f.reshape(GLA_RANK, GLA_HEADS, GLA_DK).transpose(1, 0, 2)
    wb = w_b.reshape(GLA_RANK, GLA_HEADS, GLA_DK).transpose(1, 0, 2)
    z = jnp.zeros_like(wf)
    top = jnp.concatenate([wf, z], axis=-1)
    mid = jnp.concatenate([z, wb], axis=-1)
    pad = jnp.zeros((GLA_HEADS, 128 - 2 * GLA_RANK, 2 * GLA_DK), w_f.dtype)
    wg = jnp.concatenate([top, mid, pad], axis=1).astype(BF16)
    bg = jnp.concatenate([b_f.reshape(GLA_HEADS, 1, GLA_DK), b_b.reshape(GLA_HEADS, 1, GLA_DK)], axis=-1)
    return wg, bg.astype(F32)


def _rope_tables(seq):
    rows = seq // GRID_W
    row_idx = jnp.repeat(jnp.arange(rows, dtype=F32), GRID_W)
    col_idx = jnp.tile(jnp.arange(GRID_W, dtype=F32), rows)
    inv_freq = ROPE_THETA ** (-jnp.arange(ROPE_PAIRS_PER_AXIS, dtype=F32) / ROPE_PAIRS_PER_AXIS)
    ang = jnp.concatenate([row_idx[:, None] * inv_freq, col_idx[:, None] * inv_freq], axis=-1)
    cos = jnp.cos(ang)
    sin = jnp.sin(ang)
    return jnp.concatenate([cos, cos], axis=-1), jnp.concatenate([-sin, sin], axis=-1)


def kernel(x, norm_mix, norm_ffn, gla_w_in, gla_w_gate_up_f, gla_b_gate_f, gla_w_gate_up_b, gla_b_gate_b,
           gla_norm, gla_w_out, attn_w_qkv, attn_q_norm, attn_k_norm, attn_w_out,
           ffn_w_up, ffn_w_conv, ffn_b_conv, ffn_w_down):
    batch, seq, d = x.shape
    depth = norm_mix.shape[0]
    t = batch * seq
    tm = 512
    cosf, sinf = _rope_tables(seq)
    q_scale = (HEAD_DIM ** -0.5) * math.log2(math.e)
    xf = x.reshape(t, d)
    for i in range(depth):
        j = i // 2
        nw = norm_mix[i].reshape(1, d)
        if i % 2 == 0:
            w_in = jnp.pad(gla_w_in[j], ((0, 0), (0, GLA_PROJ_COLS - gla_w_in.shape[-1]))).astype(BF16)
            wg, bg = _gla_gate_weights(gla_w_gate_up_f[j], gla_b_gate_f[j], gla_w_gate_up_b[j], gla_b_gate_b[j])
            proj = _norm_proj(xf, nw, w_in, tm)
            a = _gla(proj, wg, bg, gla_norm[j].reshape(1, GLA_DV), batch, seq)
            w_out = gla_w_out[j]
        else:
            qkv = _qkv_proj(xf, nw, attn_w_qkv[j].astype(BF16), attn_q_norm[j].reshape(1, HEAD_DIM),
                            attn_k_norm[j].reshape(1, HEAD_DIM), cosf, sinf, tm, seq, q_scale)
            a = _attention(qkv, batch, seq, tq=256, tk=min(1024, seq // 2))
            w_out = attn_w_out[j]
        xf = _mix_ffn(xf, a, w_out.astype(BF16), norm_ffn[i].reshape(1, d), ffn_w_up[i].astype(BF16),
                      ffn_w_conv[i], ffn_b_conv[i].reshape(1, 2 * D_FF), ffn_w_down[i].astype(BF16), tm, seq)
    return xf.reshape(batch, seq, d)
```

```python
import functools
import math

import jax
import jax.numpy as jnp
from jax import lax
from jax.experimental import pallas as pl
from jax.experimental.pallas import tpu as pltpu

F32 = jnp.float32
BF16 = jnp.bfloat16

D_MODEL = 1024
GRID_W = 64
NORM_EPS = 1e-6

GLA_HEADS = 4
GLA_KEY_DIM = 512
GLA_VAL_DIM = 1024
GLA_DK = 128
GLA_DV = 256
GLA_RANK = 16
GLA_GATE_NORMALIZER = 16.0
GLA_CHUNK = 64
GLA_GROUP = 4
GLA_GROUPS_PER_STEP = 4
GLA_MAIN_COLS = 2 * GLA_KEY_DIM + 2 * GLA_VAL_DIM
GLA_PROJ_COLS = GLA_MAIN_COLS + 128

HEAD_DIM = 128
Q_HEADS = 8
KV_HEADS = 2
GROUP = Q_HEADS // KV_HEADS
QKV_DIM = (Q_HEADS + 2 * KV_HEADS) * HEAD_DIM
ROPE_THETA = 10000.0
ROPE_PAIRS_PER_AXIS = HEAD_DIM // 4

D_FF = 2816
FF_CHUNK = 256
FFN_ROW_BLOCK = 64
HALO = 16

VMEM_LIMIT = 56 * 1024 * 1024


def _cparams(sem):
    return pltpu.CompilerParams(dimension_semantics=sem, vmem_limit_bytes=VMEM_LIMIT)


def _rmsnorm_rows(x, w):
    ms = jnp.mean(x * x, axis=-1, keepdims=True)
    return x * lax.rsqrt(ms + NORM_EPS) * w


def _norm_proj_kernel(x_ref, nw_ref, w_ref, o_ref, *, col_chunks):
    h = _rmsnorm_rows(x_ref[...], nw_ref[...]).astype(BF16)
    for lo, hi in col_chunks:
        o_ref[:, lo:hi] = jnp.dot(h, w_ref[:, lo:hi], preferred_element_type=F32).astype(o_ref.dtype)


def _norm_proj(x, nw, w, tm):
    t, d = x.shape
    n = w.shape[1]
    chunks = []
    lo = 0
    while lo < n:
        hi = min(lo + 1024, n)
        chunks.append((lo, hi))
        lo = hi
    return pl.pallas_call(
        functools.partial(_norm_proj_kernel, col_chunks=tuple(chunks)),
        grid=(t // tm,),
        in_specs=[
            pl.BlockSpec((tm, d), lambda i: (i, 0)),
            pl.BlockSpec((1, d), lambda i: (0, 0)),
            pl.BlockSpec((d, n), lambda i: (0, 0)),
        ],
        out_specs=pl.BlockSpec((tm, n), lambda i: (i, 0)),
        out_shape=jax.ShapeDtypeStruct((t, n), BF16),
        compiler_params=_cparams(("parallel",)),
        name="gla_in_proj",
    )(x, nw, w)


def _qkv_proj_kernel(x_ref, nw_ref, w_ref, qn_ref, kn_ref, cos_ref, sin_ref, o_ref, *, q_scale):
    h = _rmsnorm_rows(x_ref[...], nw_ref[...]).astype(BF16)
    cosf = cos_ref[...]
    sinf = sin_ref[...]
    n_rot = Q_HEADS + KV_HEADS
    n_heads = QKV_DIM // HEAD_DIM
    ys = [jnp.dot(h, w_ref[:, c * 512:(c + 1) * 512], preferred_element_type=F32) for c in range(QKV_DIM // 512)]
    heads = [ys[hd // 4][:, (hd % 4) * HEAD_DIM:(hd % 4 + 1) * HEAD_DIM] for hd in range(n_heads)]
    ms = [jnp.mean(heads[hd] * heads[hd], axis=-1, keepdims=True) for hd in range(n_rot)]
    inv = [lax.rsqrt(m + NORM_EPS) for m in ms]
    yn = [heads[hd] * inv[hd] * (qn_ref[...] if hd < Q_HEADS else kn_ref[...]) for hd in range(n_rot)]
    rot = [pltpu.roll(y, HEAD_DIM // 2, axis=1) for y in yn]
    for hd in range(n_heads):
        if hd < n_rot:
            yh = yn[hd] * cosf + rot[hd] * sinf
            if hd < Q_HEADS:
                yh = yh * q_scale
        else:
            yh = heads[hd]
        o_ref[:, hd * HEAD_DIM:(hd + 1) * HEAD_DIM] = yh.astype(o_ref.dtype)


def _qkv_proj(x, nw, w, qn, kn, cosf, sinf, tm, seq, q_scale):
    t, d = x.shape
    tiles_per_seq = seq // tm
    return pl.pallas_call(
        functools.partial(_qkv_proj_kernel, q_scale=q_scale),
        grid=(t // tm,),
        in_specs=[
            pl.BlockSpec((tm, d), lambda i: (i, 0)),
            pl.BlockSpec((1, d), lambda i: (0, 0)),
            pl.BlockSpec((d, QKV_DIM), lambda i: (0, 0)),
            pl.BlockSpec((1, HEAD_DIM), lambda i: (0, 0)),
            pl.BlockSpec((1, HEAD_DIM), lambda i: (0, 0)),
            pl.BlockSpec((tm, HEAD_DIM), lambda i: (i % tiles_per_seq, 0)),
            pl.BlockSpec((tm, HEAD_DIM), lambda i: (i % tiles_per_seq, 0)),
        ],
        out_specs=pl.BlockSpec((tm, QKV_DIM), lambda i: (i, 0)),
        out_shape=jax.ShapeDtypeStruct((t, QKV_DIM), BF16),
        compiler_params=_cparams(("parallel",)),
        name="attn_qkv_proj",
    )(x, nw, w, qn, kn, cosf, sinf)


def _log_sigmoid(x):
    return jnp.minimum(x, 0.0) - jnp.log(1.0 + jnp.exp(-jnp.abs(x)))


def _silu(x):
    return x / (1.0 + jnp.exp(-x))


def _gla_kernel(q_ref, k_ref, v_ref, g_ref, r_ref, wg_ref, bg_ref, nw_ref, o_ref,
                la_ref, oacc_ref, sf_ref, sb_ref, *, n_chunks):
    c_len = GLA_CHUNK
    n_sub = GLA_GROUP
    g_len = n_sub * c_len
    n_groups = n_chunks // n_sub
    n_par = GLA_GROUPS_PER_STEP
    logits = jnp.dot(r_ref[...], wg_ref[0], preferred_element_type=F32) + bg_ref[0]
    la_ref[...] = _log_sigmoid(logits) * (1.0 / GLA_GATE_NORMALIZER)
    sf_ref[...] = jnp.zeros_like(sf_ref)
    sb_ref[...] = jnp.zeros_like(sb_ref)

    row = lax.broadcasted_iota(jnp.int32, (g_len, g_len), 0)
    col = lax.broadcasted_iota(jnp.int32, (g_len, g_len), 1)
    same_chunk = (row // c_len) == (col // c_len)
    cum_prefix = jnp.where(same_chunk & (row >= col), 1.0, 0.0).astype(BF16)
    q_scale = GLA_DK ** -0.5
    nw = nw_ref[...]
    trans_b = (((1,), (1,)), ((), ()))
    trans_a = (((0,), (0,)), ((), ()))

    def step(i, final):
        work = []
        for forward in (True, False):
            for s in range(n_par):
                gi = i * n_par + s
                g = gi if forward else n_groups - 1 - gi
                base = pl.multiple_of(g * g_len, g_len)
                if forward:
                    lanes, keep = slice(0, GLA_DK), row >= col
                else:
                    lanes, keep = slice(GLA_DK, 2 * GLA_DK), col > row
                work.append(dict(forward=forward, base=base, rows=pl.ds(base, g_len), lanes=lanes,
                                 mask=same_chunk & keep))

        n_half = len(work) // 2
        for fw, bw in zip(work[:n_half], work[n_half:]):
            la = jnp.concatenate([la_ref[fw["rows"], fw["lanes"]], la_ref[bw["rows"], bw["lanes"]]], axis=1)
            la_hi = la.astype(BF16)
            la_lo = (la - la_hi.astype(F32)).astype(BF16)
            pre = (jnp.dot(cum_prefix, la_hi, preferred_element_type=F32)
                   + jnp.dot(cum_prefix, la_lo, preferred_element_type=F32))
            fw["b"] = pre[:, :GLA_DK]
            pre_b = pre[:, GLA_DK:]
            tot = jnp.concatenate(
                [jnp.broadcast_to(pre_b[u * c_len + c_len - 1:u * c_len + c_len, :], (c_len, GLA_DK))
                 for u in range(n_sub)], axis=0)
            bw["b"] = tot - pre_b + la[:, GLA_DK:]

        for d in work:
            b = d["b"]
            ends = []
            for u in range(n_sub):
                r = u * c_len + (c_len - 1 if d["forward"] else 0)
                ends.append(b[r:r + 1, :])
            b_end = jnp.concatenate([jnp.broadcast_to(e, (c_len, GLA_DK)) for e in ends], axis=0)
            q = q_ref[d["rows"], :].astype(F32)
            k = k_ref[d["rows"], :].astype(F32)
            d["v"] = v_ref[d["rows"], :]
            d["q_dec"] = (q * (jnp.exp(b) * q_scale)).astype(BF16)
            k_inv = (k * jnp.exp(-b)).astype(BF16)
            d["k_out"] = (k * jnp.exp(b_end - b)).astype(BF16)
            d["decay"] = [jnp.exp(e) for e in ends]
            att = lax.dot_general(d["q_dec"], k_inv, trans_b, preferred_element_type=F32)
            d["att"] = jnp.where(d["mask"], att, 0.0).astype(BF16)

        for d in work:
            d["o"] = jnp.dot(d["att"], d["v"], preferred_element_type=F32)
            d["kv"] = []
            for u in range(n_sub):
                sl = slice(u * c_len, (u + 1) * c_len)
                d["kv"].append(lax.dot_general(d["k_out"][sl, :], d["v"][sl, :], trans_a,
                                               preferred_element_type=F32))

        for forward in (True, False):
            st_ref = sf_ref if forward else sb_ref
            state = st_ref[...]
            for d in (w for w in work if w["forward"] == forward):
                for u in (range(n_sub) if forward else range(n_sub - 1, -1, -1)):
                    sl = slice(u * c_len, (u + 1) * c_len)
                    o_u = d["o"][sl, :] + jnp.dot(d["q_dec"][sl, :], state.astype(BF16),
                                                  preferred_element_type=F32)
                    dec_col = jnp.broadcast_to(d["decay"][u], (GLA_DK, GLA_DK)).T
                    state = state * jnp.concatenate([dec_col] * (GLA_DV // GLA_DK), axis=1) + d["kv"][u]
                    rows_u = pl.ds(pl.multiple_of(d["base"] + u * c_len, c_len), c_len)
                    if final:
                        o_sum = o_u + oacc_ref[rows_u, :]
                        g = g_ref[rows_u, :].astype(F32)
                        o_ref[rows_u, :] = (_rmsnorm_rows(o_sum, nw) * _silu(g)).astype(o_ref.dtype)
                    else:
                        oacc_ref[rows_u, :] = o_u
            st_ref[...] = state

    n_steps = n_groups // n_par
    half = n_steps // 2

    def first_half(i, carry):
        step(i, False)
        return carry

    def second_half(i, carry):
        step(i, True)
        return carry

    lax.fori_loop(0, half, first_half, 0)
    lax.fori_loop(half, n_steps, second_half, 0)


def _gla(proj, wg, bg, nw, batch, seq):
    t = batch * seq
    n_chunks = seq // GLA_CHUNK
    assert n_chunks % (2 * GLA_GROUP * GLA_GROUPS_PER_STEP) == 0
    kq = GLA_KEY_DIM // GLA_DK
    return pl.pallas_call(
        functools.partial(_gla_kernel, n_chunks=n_chunks),
        grid=(batch, GLA_HEADS),
        in_specs=[
            pl.BlockSpec((seq, GLA_DK), lambda b, h: (b, h)),
            pl.BlockSpec((seq, GLA_DK), lambda b, h: (b, kq + h)),
            pl.BlockSpec((seq, GLA_DV), lambda b, h: (b, (2 * GLA_KEY_DIM) // GLA_DV + h)),
            pl.BlockSpec((seq, GLA_DV), lambda b, h: (b, (2 * GLA_KEY_DIM + GLA_VAL_DIM) // GLA_DV + h)),
            pl.BlockSpec((seq, 128), lambda b, h: (b, GLA_MAIN_COLS // 128)),
            pl.BlockSpec((1, 128, 2 * GLA_DK), lambda b, h: (h, 0, 0)),
            pl.BlockSpec((1, 1, 2 * GLA_DK), lambda b, h: (h, 0, 0)),
            pl.BlockSpec((1, GLA_DV), lambda b, h: (0, 0)),
        ],
        out_specs=pl.BlockSpec((seq, GLA_DV), lambda b, h: (b, h)),
        out_shape=jax.ShapeDtypeStruct((t, GLA_VAL_DIM), BF16),
        scratch_shapes=[
            pltpu.VMEM((seq, 2 * GLA_DK), F32),
            pltpu.VMEM((seq, GLA_DV), F32),
            pltpu.VMEM((GLA_DK, GLA_DV), F32),
            pltpu.VMEM((GLA_DK, GLA_DV), F32),
        ],
        compiler_params=_cparams(("parallel", "parallel")),
        name="gla_scan",
    )(proj, proj, proj, proj, proj, wg, bg, nw)


def _attn_kernel(q_ref, k_ref, v_ref, o_ref, s_ref, m_ref, acc_ref, va_ref, *, tq, tk, n_q, n_kv):
    assert n_kv % 2 == 0

    def scores(tile, blk, slot):
        rows = pl.ds(pl.multiple_of(tile * tq, tq), tq)
        kb = k_ref[blk * tk:(blk + 1) * tk, :]
        for g in range(GROUP):
            qg = q_ref[rows, g * HEAD_DIM:(g + 1) * HEAD_DIM]
            s_ref[slot, g * tq:(g + 1) * tq, :] = lax.dot_general(
                qg, kb, (((1,), (1,)), ((), ())), preferred_element_type=F32)

    def softmax_pv(tile, blk, slot):
        vb = va_ref[blk * tk:(blk + 1) * tk, :]
        for g in range(GROUP):
            gr = slice(g * tq, (g + 1) * tq)
            s = s_ref[slot, gr, :]
            m_blk = jnp.max(s, axis=-1, keepdims=True)
            if blk == 0:
                m_new = m_blk
            else:
                m_old = m_ref[gr, :]
                m_new = jnp.maximum(m_old, m_blk)
                alpha = jnp.exp2(m_old - m_new)
            p = jnp.exp2(s - m_new).astype(BF16)
            pv = jnp.dot(p, vb, preferred_element_type=F32)
            acc_new = pv if blk == 0 else alpha * acc_ref[gr, :] + pv
            if blk == n_kv - 1:
                rows = pl.ds(pl.multiple_of(tile * tq, tq), tq)
                o_ref[rows, g * HEAD_DIM:(g + 1) * HEAD_DIM] = (
                    acc_new[:, :HEAD_DIM] / acc_new[:, HEAD_DIM:]).astype(o_ref.dtype)
            else:
                m_ref[gr, :] = m_new
                acc_ref[gr, :] = acc_new

    va_ref[:, :HEAD_DIM] = v_ref[...]
    va_ref[:, HEAD_DIM:] = jnp.ones((va_ref.shape[0], HEAD_DIM), BF16)
    scores(0, 0, 0)

    def tile_body(qi, carry):
        for j in range(n_kv):
            if j + 1 < n_kv:
                scores(qi, j + 1, (j + 1) % 2)
            else:
                scores(jnp.minimum(qi + 1, n_q - 1), 0, 0)
            softmax_pv(qi, j, j % 2)
        return carry

    lax.fori_loop(0, n_q, tile_body, 0)


def _attention(qkv, batch, seq, tq, tk):
    t = batch * seq
    gw = GROUP * HEAD_DIM
    rows = GROUP * tq
    return pl.pallas_call(
        functools.partial(_attn_kernel, tq=tq, tk=tk, n_q=seq // tq, n_kv=seq // tk),
        grid=(batch, KV_HEADS),
        in_specs=[
            pl.BlockSpec((seq, gw), lambda b, h: (b, h)),
            pl.BlockSpec((seq, HEAD_DIM), lambda b, h: (b, Q_HEADS + h)),
            pl.BlockSpec((seq, HEAD_DIM), lambda b, h: (b, Q_HEADS + KV_HEADS + h)),
        ],
        out_specs=pl.BlockSpec((seq, gw), lambda b, h: (b, h)),
        out_shape=jax.ShapeDtypeStruct((t, Q_HEADS * HEAD_DIM), BF16),
        scratch_shapes=[
            pltpu.VMEM((2, rows, tk), F32),
            pltpu.VMEM((rows, 1), F32),
            pltpu.VMEM((rows, 2 * HEAD_DIM), F32),
            pltpu.VMEM((seq, 2 * HEAD_DIM), BF16),
        ],
        compiler_params=_cparams(("parallel", "parallel")),
        name="gqa_attention",
    )(qkv, qkv, qkv)


def _ffn_kernel(x_ref, xp_ref, xn_ref, a_ref, ap_ref, an_ref, wo_ref, nw_ref, wu_ref, wc_ref, bc_ref, wd_ref,
                o_ref, h_ref, u_ref, act_ref, acc_ref, *, tm, tiles_per_seq):
    i = pl.program_id(0)
    nw = nw_ref[...]
    wo = wo_ref[...]

    def mixed(xr, ar):
        return xr[...] + jnp.dot(ar[...], wo, preferred_element_type=F32)

    x = mixed(x_ref, a_ref)
    keep_prev = jnp.where(i % tiles_per_seq == 0, 0.0, 1.0)
    keep_next = jnp.where(i % tiles_per_seq == tiles_per_seq - 1, 0.0, 1.0)
    h_ref[0:HALO, :] = (_rmsnorm_rows(mixed(xp_ref, ap_ref), nw) * keep_prev).astype(BF16)
    h_ref[HALO:HALO + tm, :] = _rmsnorm_rows(x, nw).astype(BF16)
    h_ref[HALO + tm:2 * HALO + tm, :] = (_rmsnorm_rows(mixed(xn_ref, an_ref), nw) * keep_next).astype(BF16)
    acc_ref[...] = x
    fc = FF_CHUNK
    n_chunks = D_FF // fc
    rb = FFN_ROW_BLOCK
    win = rb + 16

    ext = tm + 2 * HALO
    up_rows = [(0, 144), (144, 272), (272, 400), (400, ext)]
    down_rows = [(k * (tm // 4), (k + 1) * (tm // 4)) for k in range(4)]

    def up_piece(c, part, lo, hi):
        col0 = part * D_FF + c * fc
        u_ref[c % 2, part, lo:hi, :] = jnp.dot(h_ref[lo:hi, :], wu_ref[:, col0:col0 + fc],
                                               preferred_element_type=F32)

    def down_piece(c, lo, hi):
        acc_ref[lo:hi, :] += jnp.dot(act_ref[c % 2, lo:hi, :], wd_ref[c * fc:(c + 1) * fc, :],
                                     preferred_element_type=F32)

    def conv(slot, part, r, col0):
        uw = u_ref[slot, part, HALO - 8 + r * rb:HALO + 8 + (r + 1) * rb, :]
        wc = wc_ref[:, col0:col0 + fc]
        u_prev = pltpu.roll(uw, 1, axis=0)[8:8 + rb, :]
        u_next = pltpu.roll(uw, win - 1, axis=0)[8:8 + rb, :]
        return (u_prev * wc[0:1, :] + uw[8:8 + rb, :] * wc[1:2, :] + u_next * wc[2:3, :]
                + bc_ref[:, col0:col0 + fc])

    def elem_block(c, r):
        val = conv(c % 2, 0, r, c * fc)
        gate = conv(c % 2, 1, r, D_FF + c * fc)
        act_ref[c % 2, r * rb:(r + 1) * rb, :] = (_silu(gate) * val).astype(BF16)

    def stage(c):
        pieces = []
        if c + 1 < n_chunks:
            for lo, hi in up_rows:
                pieces.append(functools.partial(up_piece, c + 1, 0, lo, hi))
                pieces.append(functools.partial(up_piece, c + 1, 1, lo, hi))
        if c > 0:
            for lo, hi in down_rows:
                pieces.append(functools.partial(down_piece, c - 1, lo, hi))
        n_blocks = tm // rb
        done = 0
        for k, piece in enumerate(pieces):
            piece()
            upto = (k + 1) * n_blocks // len(pieces)
            for r in range(done, upto):
                elem_block(c, r)
            done = upto

    for lo, hi in up_rows:
        up_piece(0, 0, lo, hi)
        up_piece(0, 1, lo, hi)
    for c in range(n_chunks):
        stage(c)
    for lo, hi in down_rows:
        down_piece(n_chunks - 1, lo, hi)
    o_ref[...] = acc_ref[...]


def _mix_ffn(x, a, wo, nw, wu, wc, bc, wd, tm, seq):
    t, d = x.shape
    ka = a.shape[1]
    tiles_per_seq = seq // tm
    hb = tm // HALO
    n_hblk = t // HALO
    main = lambda i: (i, 0)
    prev = lambda i: (jnp.maximum(i * hb - 1, 0), 0)
    nxt = lambda i: (jnp.minimum((i + 1) * hb, n_hblk - 1), 0)
    const = lambda i: (0, 0)
    return pl.pallas_call(
        functools.partial(_ffn_kernel, tm=tm, tiles_per_seq=tiles_per_seq),
        grid=(t // tm,),
        in_specs=[
            pl.BlockSpec((tm, d), main),
            pl.BlockSpec((HALO, d), prev),
            pl.BlockSpec((HALO, d), nxt),
            pl.BlockSpec((tm, ka), main),
            pl.BlockSpec((HALO, ka), prev),
            pl.BlockSpec((HALO, ka), nxt),
            pl.BlockSpec((ka, d), const),
            pl.BlockSpec((1, d), const),
            pl.BlockSpec((d, 2 * D_FF), const),
            pl.BlockSpec((3, 2 * D_FF), const),
            pl.BlockSpec((1, 2 * D_FF), const),
            pl.BlockSpec((D_FF, d), const),
        ],
        out_specs=pl.BlockSpec((tm, d), main),
        out_shape=jax.ShapeDtypeStruct((t, d), F32),
        scratch_shapes=[
            pltpu.VMEM((tm + 2 * HALO, d), BF16),
            pltpu.VMEM((2, 2, tm + 2 * HALO, FF_CHUNK), F32),
            pltpu.VMEM((2, tm, FF_CHUNK), BF16),
            pltpu.VMEM((tm, d), F32),
        ],
        compiler_params=_cparams(("parallel",)),
        name="mix_conv_ffn",
    )(x, x, x, a, a, a, wo, nw, wu, wc, bc, wd)


def _gla_gate_weights(w_f, b_f, w_b, b_b):
    wf = w_f.reshape(GLA_RANK, GLA_HEADS, GLA_DK).transpose(1, 0, 2)
    wb = w_b.reshape(GLA_RANK, GLA_HEADS, GLA_DK).transpose(1, 0, 2)
    z = jnp.zeros_like(wf)
    top = jnp.concatenate([wf, z], axis=-1)
    mid = jnp.concatenate([z, wb], axis=-1)
    pad = jnp.zeros((GLA_HEADS, 128 - 2 * GLA_RANK, 2 * GLA_DK), w_f.dtype)
    wg = jnp.concatenate([top, mid, pad], axis=1).astype(BF16)
    bg = jnp.concatenate([b_f.reshape(GLA_HEADS, 1, GLA_DK), b_b.reshape(GLA_HEADS, 1, GLA_DK)], axis=-1)
    return wg, bg.astype(F32)


def _rope_tables(seq):
    rows = seq // GRID_W
    row_idx = jnp.repeat(jnp.arange(rows, dtype=F32), GRID_W)
    col_idx = jnp.tile(jnp.arange(GRID_W, dtype=F32), rows)
    inv_freq = ROPE_THETA ** (-jnp.arange(ROPE_PAIRS_PER_AXIS, dtype=F32) / ROPE_PAIRS_PER_AXIS)
    ang = jnp.concatenate([row_idx[:, None] * inv_freq, col_idx[:, None] * inv_freq], axis=-1)
    cos = jnp.cos(ang)
    sin = jnp.sin(ang)
    return jnp.concatenate([cos, cos], axis=-1), jnp.concatenate([-sin, sin], axis=-1)


def kernel(x, norm_mix, norm_ffn, gla_w_in, gla_w_gate_up_f, gla_b_gate_f, gla_w_gate_up_b, gla_b_gate_b,
           gla_norm, gla_w_out, attn_w_qkv, attn_q_norm, attn_k_norm, attn_w_out,
           ffn_w_up, ffn_w_conv, ffn_b_conv, ffn_w_down):
    batch, seq, d = x.shape
    depth = norm_mix.shape[0]
    t = batch * seq
    tm = 512
    cosf, sinf = _rope_tables(seq)
    q_scale = (HEAD_DIM ** -0.5) * math.log2(math.e)
    xf = x.reshape(t, d)
    for i in range(depth):
        j = i // 2
        nw = norm_mix[i].reshape(1, d)
        if i % 2 == 0:
            w_in = jnp.pad(gla_w_in[j], ((0, 0), (0, GLA_PROJ_COLS - gla_w_in.shape[-1]))).astype(BF16)
            wg, bg = _gla_gate_weights(gla_w_gate_up_f[j], gla_b_gate_f[j], gla_w_gate_up_b[j], gla_b_gate_b[j])
            proj = _norm_proj(xf, nw, w_in, tm)
            a = _gla(proj, wg, bg, gla_norm[j].reshape(1, GLA_DV), batch, seq)
            w_out = gla_w_out[j]
        else:
            qkv = _qkv_proj(xf, nw, attn_w_qkv[j].astype(BF16), attn_q_norm[j].reshape(1, HEAD_DIM),
                            attn_k_norm[j].reshape(1, HEAD_DIM), cosf, sinf, tm, seq, q_scale)
            a = _attention(qkv, batch, seq, tq=256, tk=min(1024, seq // 2))
            w_out = attn_w_out[j]
        xf = _mix_ffn(xf, a, w_out.astype(BF16), norm_ffn[i].reshape(1, d), ffn_w_up[i].astype(BF16),
                      ffn_w_conv[i], ffn_b_conv[i].reshape(1, 2 * D_FF), ffn_w_down[i].astype(BF16), tm, seq)
    return xf.reshape(batch, seq, d)
```

```python
import functools
import math

import jax
import jax.numpy as jnp
from jax import lax
from jax.experimental import pallas as pl
from jax.experimental.pallas import tpu as pltpu

F32 = jnp.float32
BF16 = jnp.bfloat16

D_MODEL = 1024
GRID_W = 64
NORM_EPS = 1e-6

GLA_HEADS = 4
GLA_KEY_DIM = 512
GLA_VAL_DIM = 1024
GLA_DK = 128
GLA_DV = 256
GLA_RANK = 16
GLA_GATE_NORMALIZER = 16.0
GLA_CHUNK = 64
GLA_GROUP = 4
GLA_GROUPS_PER_STEP = 4
GLA_MAIN_COLS = 2 * GLA_KEY_DIM + 2 * GLA_VAL_DIM
GLA_PROJ_COLS = GLA_MAIN_COLS + 128

HEAD_DIM = 128
Q_HEADS = 8
KV_HEADS = 2
GROUP = Q_HEADS // KV_HEADS
QKV_DIM = (Q_HEADS + 2 * KV_HEADS) * HEAD_DIM
ROPE_THETA = 10000.0
ROPE_PAIRS_PER_AXIS = HEAD_DIM // 4

D_FF = 2816
FF_CHUNK = 256
FFN_ROW_BLOCK = 64
HALO = 16

VMEM_LIMIT = 56 * 1024 * 1024


def _cparams(sem):
    return pltpu.CompilerParams(dimension_semantics=sem, vmem_limit_bytes=VMEM_LIMIT)


def _rmsnorm_rows(x, w):
    ms = jnp.mean(x * x, axis=-1, keepdims=True)
    return x * lax.rsqrt(ms + NORM_EPS) * w


def _norm_proj_kernel(x_ref, nw_ref, w_ref, o_ref, *, col_chunks):
    h = _rmsnorm_rows(x_ref[...], nw_ref[...]).astype(BF16)
    for lo, hi in col_chunks:
        o_ref[:, lo:hi] = jnp.dot(h, w_ref[:, lo:hi], preferred_element_type=F32).astype(o_ref.dtype)


def _norm_proj(x, nw, w, layer, tm):
    t, d = x.shape
    n = w.shape[-1]
    chunks = []
    lo = 0
    while lo < n:
        hi = min(lo + 1024, n)
        chunks.append((lo, hi))
        lo = hi
    return pl.pallas_call(
        functools.partial(_norm_proj_kernel, col_chunks=tuple(chunks)),
        grid=(t // tm,),
        in_specs=[
            pl.BlockSpec((tm, d), lambda i: (i, 0)),
            pl.BlockSpec((1, d), lambda i: (0, 0)),
            pl.BlockSpec((None, d, n), lambda i: (layer, 0, 0), pipeline_mode=pl.Buffered(1)),
        ],
        out_specs=pl.BlockSpec((tm, n), lambda i: (i, 0)),
        out_shape=jax.ShapeDtypeStruct((t, n), BF16),
        compiler_params=_cparams(("parallel",)),
        name="gla_in_proj",
    )(x, nw, w)


def _qkv_proj_kernel(x_ref, nw_ref, w_ref, qn_ref, kn_ref, cos_ref, sin_ref, o_ref, *, q_scale):
    h = _rmsnorm_rows(x_ref[...], nw_ref[...]).astype(BF16)
    cosf = cos_ref[...]
    sinf = sin_ref[...]
    n_rot = Q_HEADS + KV_HEADS
    n_heads = QKV_DIM // HEAD_DIM
    ys = [jnp.dot(h, w_ref[:, c * 512:(c + 1) * 512], preferred_element_type=F32) for c in range(QKV_DIM // 512)]
    heads = [ys[hd // 4][:, (hd % 4) * HEAD_DIM:(hd % 4 + 1) * HEAD_DIM] for hd in range(n_heads)]
    ms = [jnp.mean(heads[hd] * heads[hd], axis=-1, keepdims=True) for hd in range(n_rot)]
    inv = [lax.rsqrt(m + NORM_EPS) for m in ms]
    yn = [heads[hd] * inv[hd] * (qn_ref[...] if hd < Q_HEADS else kn_ref[...]) for hd in range(n_rot)]
    rot = [pltpu.roll(y, HEAD_DIM // 2, axis=1) for y in yn]
    for hd in range(n_heads):
        if hd < n_rot:
            yh = yn[hd] * cosf + rot[hd] * sinf
            if hd < Q_HEADS:
                yh = yh * q_scale
        else:
            yh = heads[hd]
        o_ref[:, hd * HEAD_DIM:(hd + 1) * HEAD_DIM] = yh.astype(o_ref.dtype)


def _qkv_proj(x, nw, w, layer, qn, kn, cosf, sinf, tm, seq, q_scale):
    t, d = x.shape
    tiles_per_seq = seq // tm
    return pl.pallas_call(
        functools.partial(_qkv_proj_kernel, q_scale=q_scale),
        grid=(t // tm,),
        in_specs=[
            pl.BlockSpec((tm, d), lambda i: (i, 0)),
            pl.BlockSpec((1, d), lambda i: (0, 0)),
            pl.BlockSpec((None, d, QKV_DIM), lambda i: (layer, 0, 0), pipeline_mode=pl.Buffered(1)),
            pl.BlockSpec((1, HEAD_DIM), lambda i: (0, 0)),
            pl.BlockSpec((1, HEAD_DIM), lambda i: (0, 0)),
            pl.BlockSpec((tm, HEAD_DIM), lambda i: (i % tiles_per_seq, 0)),
            pl.BlockSpec((tm, HEAD_DIM), lambda i: (i % tiles_per_seq, 0)),
        ],
        out_specs=pl.BlockSpec((tm, QKV_DIM), lambda i: (i, 0)),
        out_shape=jax.ShapeDtypeStruct((t, QKV_DIM), BF16),
        compiler_params=_cparams(("parallel",)),
        name="attn_qkv_proj",
    )(x, nw, w, qn, kn, cosf, sinf)


def _log_sigmoid(x):
    return jnp.minimum(x, 0.0) - jnp.log(1.0 + jnp.exp(-jnp.abs(x)))


def _silu(x):
    return x / (1.0 + jnp.exp(-x))


def _gla_kernel(q_ref, k_ref, v_ref, g_ref, r_ref, wg_ref, bg_ref, nw_ref, o_ref,
                la_ref, oacc_ref, sf_ref, sb_ref, *, n_chunks):
    c_len = GLA_CHUNK
    n_sub = GLA_GROUP
    g_len = n_sub * c_len
    n_groups = n_chunks // n_sub
    n_par = GLA_GROUPS_PER_STEP
    logits = jnp.dot(r_ref[...], wg_ref[0], preferred_element_type=F32) + bg_ref[0]
    la_ref[...] = _log_sigmoid(logits) * (1.0 / GLA_GATE_NORMALIZER)
    sf_ref[...] = jnp.zeros_like(sf_ref)
    sb_ref[...] = jnp.zeros_like(sb_ref)

    row = lax.broadcasted_iota(jnp.int32, (g_len, g_len), 0)
    col = lax.broadcasted_iota(jnp.int32, (g_len, g_len), 1)
    same_chunk = (row // c_len) == (col // c_len)
    cum_prefix = jnp.where(same_chunk & (row >= col), 1.0, 0.0).astype(BF16)
    q_scale = GLA_DK ** -0.5
    nw = nw_ref[...]
    trans_b = (((1,), (1,)), ((), ()))
    trans_a = (((0,), (0,)), ((), ()))

    def step(i, final):
        work = []
        for forward in (True, False):
            for s in range(n_par):
                gi = i * n_par + s
                g = gi if forward else n_groups - 1 - gi
                base = pl.multiple_of(g * g_len, g_len)
                if forward:
                    lanes, keep = slice(0, GLA_DK), row >= col
                else:
                    lanes, keep = slice(GLA_DK, 2 * GLA_DK), col > row
                work.append(dict(forward=forward, base=base, rows=pl.ds(base, g_len), lanes=lanes,
                                 mask=same_chunk & keep))

        n_half = len(work) // 2
        for fw, bw in zip(work[:n_half], work[n_half:]):
            la = jnp.concatenate([la_ref[fw["rows"], fw["lanes"]], la_ref[bw["rows"], bw["lanes"]]], axis=1)
            la_hi = la.astype(BF16)
            la_lo = (la - la_hi.astype(F32)).astype(BF16)
            pre = (jnp.dot(cum_prefix, la_hi, preferred_element_type=F32)
                   + jnp.dot(cum_prefix, la_lo, preferred_element_type=F32))
            fw["b"] = pre[:, :GLA_DK]
            pre_b = pre[:, GLA_DK:]
            tot = jnp.concatenate(
                [jnp.broadcast_to(pre_b[u * c_len + c_len - 1:u * c_len + c_len, :], (c_len, GLA_DK))
                 for u in range(n_sub)], axis=0)
            bw["b"] = tot - pre_b + la[:, GLA_DK:]

        for d in work:
            b = d["b"]
            ends = []
            for u in range(n_sub):
                r = u * c_len + (c_len - 1 if d["forward"] else 0)
                ends.append(b[r:r + 1, :])
            b_end = jnp.concatenate([jnp.broadcast_to(e, (c_len, GLA_DK)) for e in ends], axis=0)
            q = q_ref[d["rows"], :].astype(F32)
            k = k_ref[d["rows"], :].astype(F32)
            d["v"] = v_ref[d["rows"], :]
            d["q_dec"] = (q * (jnp.exp(b) * q_scale)).astype(BF16)
            k_inv = (k * jnp.exp(-b)).astype(BF16)
            d["k_out"] = (k * jnp.exp(b_end - b)).astype(BF16)
            d["decay"] = [jnp.exp(e) for e in ends]
            att = lax.dot_general(d["q_dec"], k_inv, trans_b, preferred_element_type=F32)
            d["att"] = jnp.where(d["mask"], att, 0.0).astype(BF16)

        for d in work:
            d["o"] = jnp.dot(d["att"], d["v"], preferred_element_type=F32)
            d["kv"] = []
            for u in range(n_sub):
                sl = slice(u * c_len, (u + 1) * c_len)
                d["kv"].append(lax.dot_general(d["k_out"][sl, :], d["v"][sl, :], trans_a,
                                               preferred_element_type=F32))

        for forward in (True, False):
            st_ref = sf_ref if forward else sb_ref
            state = st_ref[...]
            for d in (w for w in work if w["forward"] == forward):
                for u in (range(n_sub) if forward else range(n_sub - 1, -1, -1)):
                    sl = slice(u * c_len, (u + 1) * c_len)
                    o_u = d["o"][sl, :] + jnp.dot(d["q_dec"][sl, :], state.astype(BF16),
                                                  preferred_element_type=F32)
                    dec_col = jnp.broadcast_to(d["decay"][u], (GLA_DK, GLA_DK)).T
                    state = state * jnp.concatenate([dec_col] * (GLA_DV // GLA_DK), axis=1) + d["kv"][u]
                    rows_u = pl.ds(pl.multiple_of(d["base"] + u * c_len, c_len), c_len)
                    if final:
                        o_sum = o_u + oacc_ref[rows_u, :]
                        g = g_ref[rows_u, :].astype(F32)
                        o_ref[rows_u, :] = (_rmsnorm_rows(o_sum, nw) * _silu(g)).astype(o_ref.dtype)
                    else:
                        oacc_ref[rows_u, :] = o_u
            st_ref[...] = state

    n_steps = n_groups // n_par
    half = n_steps // 2

    def first_half(i, carry):
        step(i, False)
        return carry

    def second_half(i, carry):
        step(i, True)
        return carry

    lax.fori_loop(0, half, first_half, 0)
    lax.fori_loop(half, n_steps, second_half, 0)


def _gla(proj, wg, bg, nw, batch, seq):
    t = batch * seq
    n_chunks = seq // GLA_CHUNK
    assert n_chunks % (2 * GLA_GROUP * GLA_GROUPS_PER_STEP) == 0
    kq = GLA_KEY_DIM // GLA_DK
    return pl.pallas_call(
        functools.partial(_gla_kernel, n_chunks=n_chunks),
        grid=(batch, GLA_HEADS),
        in_specs=[
            pl.BlockSpec((seq, GLA_DK), lambda b, h: (b, h)),
            pl.BlockSpec((seq, GLA_DK), lambda b, h: (b, kq + h)),
            pl.BlockSpec((seq, GLA_DV), lambda b, h: (b, (2 * GLA_KEY_DIM) // GLA_DV + h)),
            pl.BlockSpec((seq, GLA_DV), lambda b, h: (b, (2 * GLA_KEY_DIM + GLA_VAL_DIM) // GLA_DV + h)),
            pl.BlockSpec((seq, 128), lambda b, h: (b, GLA_MAIN_COLS // 128)),
            pl.BlockSpec((1, 128, 2 * GLA_DK), lambda b, h: (h, 0, 0)),
            pl.BlockSpec((1, 1, 2 * GLA_DK), lambda b, h: (h, 0, 0)),
            pl.BlockSpec((1, GLA_DV), lambda b, h: (0, 0)),
        ],
        out_specs=pl.BlockSpec((seq, GLA_DV), lambda b, h: (b, h)),
        out_shape=jax.ShapeDtypeStruct((t, GLA_VAL_DIM), BF16),
        scratch_shapes=[
            pltpu.VMEM((seq, 2 * GLA_DK), F32),
            pltpu.VMEM((seq, GLA_DV), F32),
            pltpu.VMEM((GLA_DK, GLA_DV), F32),
            pltpu.VMEM((GLA_DK, GLA_DV), F32),
        ],
        compiler_params=_cparams(("parallel", "parallel")),
        name="gla_scan",
    )(proj, proj, proj, proj, proj, wg, bg, nw)


def _attn_kernel(q_ref, k_ref, v_ref, o_ref, s_ref, m_ref, acc_ref, va_ref, *, tq, tk, n_q, n_kv):
    assert n_kv % 2 == 0

    def scores(tile, blk, slot):
        rows = pl.ds(pl.multiple_of(tile * tq, tq), tq)
        kb = k_ref[blk * tk:(blk + 1) * tk, :]
        for g in range(GROUP):
            qg = q_ref[rows, g * HEAD_DIM:(g + 1) * HEAD_DIM]
            s_ref[slot, g * tq:(g + 1) * tq, :] = lax.dot_general(
                qg, kb, (((1,), (1,)), ((), ())), preferred_element_type=F32)

    def softmax_pv(tile, blk, slot):
        vb = va_ref[blk * tk:(blk + 1) * tk, :]
        for g in range(GROUP):
            gr = slice(g * tq, (g + 1) * tq)
            s = s_ref[slot, gr, :]
            m_blk = jnp.max(s, axis=-1, keepdims=True)
            if blk == 0:
                m_new = m_blk
            else:
                m_old = m_ref[gr, :]
                m_new = jnp.maximum(m_old, m_blk)
                alpha = jnp.exp2(m_old - m_new)
            p = jnp.exp2(s - m_new).astype(BF16)
            pv = jnp.dot(p, vb, preferred_element_type=F32)
            acc_new = pv if blk == 0 else alpha * acc_ref[gr, :] + pv
            if blk == n_kv - 1:
                rows = pl.ds(pl.multiple_of(tile * tq, tq), tq)
                o_ref[rows, g * HEAD_DIM:(g + 1) * HEAD_DIM] = (
                    acc_new[:, :HEAD_DIM] / acc_new[:, HEAD_DIM:]).astype(o_ref.dtype)
            else:
                m_ref[gr, :] = m_new
                acc_ref[gr, :] = acc_new

    va_ref[:, :HEAD_DIM] = v_ref[...]
    va_ref[:, HEAD_DIM:] = jnp.ones((va_ref.shape[0], HEAD_DIM), BF16)
    scores(0, 0, 0)

    def tile_body(qi, carry):
        for j in range(n_kv):
            if j + 1 < n_kv:
                scores(qi, j + 1, (j + 1) % 2)
            else:
                scores(jnp.minimum(qi + 1, n_q - 1), 0, 0)
            softmax_pv(qi, j, j % 2)
        return carry

    lax.fori_loop(0, n_q, tile_body, 0)


def _attention(qkv, batch, seq, tq, tk):
    t = batch * seq
    gw = GROUP * HEAD_DIM
    rows = GROUP * tq
    return pl.pallas_call(
        functools.partial(_attn_kernel, tq=tq, tk=tk, n_q=seq // tq, n_kv=seq // tk),
        grid=(batch, KV_HEADS),
        in_specs=[
            pl.BlockSpec((seq, gw), lambda b, h: (b, h)),
            pl.BlockSpec((seq, HEAD_DIM), lambda b, h: (b, Q_HEADS + h)),
            pl.BlockSpec((seq, HEAD_DIM), lambda b, h: (b, Q_HEADS + KV_HEADS + h)),
        ],
        out_specs=pl.BlockSpec((seq, gw), lambda b, h: (b, h)),
        out_shape=jax.ShapeDtypeStruct((t, Q_HEADS * HEAD_DIM), BF16),
        scratch_shapes=[
            pltpu.VMEM((2, rows, tk), F32),
            pltpu.VMEM((rows, 1), F32),
            pltpu.VMEM((rows, 2 * HEAD_DIM), F32),
            pltpu.VMEM((seq, 2 * HEAD_DIM), BF16),
        ],
        compiler_params=_cparams(("parallel", "parallel")),
        name="gqa_attention",
    )(qkv, qkv, qkv)


def _ffn_kernel(x_ref, xp_ref, xn_ref, a_ref, ap_ref, an_ref, wo_ref, nw_ref, wu_ref, wc_ref, bc_ref, wd_ref,
                o_ref, h_ref, u_ref, act_ref, acc_ref, *, tm, tiles_per_seq):
    i = pl.program_id(0)
    nw = nw_ref[...]
    wo = wo_ref[...]

    def mixed(xr, ar):
        return xr[...] + jnp.dot(ar[...], wo, preferred_element_type=F32)

    x = mixed(x_ref, a_ref)
    keep_prev = jnp.where(i % tiles_per_seq == 0, 0.0, 1.0)
    keep_next = jnp.where(i % tiles_per_seq == tiles_per_seq - 1, 0.0, 1.0)
    h_ref[0:HALO, :] = (_rmsnorm_rows(mixed(xp_ref, ap_ref), nw) * keep_prev).astype(BF16)
    h_ref[HALO:HALO + tm, :] = _rmsnorm_rows(x, nw).astype(BF16)
    h_ref[HALO + tm:2 * HALO + tm, :] = (_rmsnorm_rows(mixed(xn_ref, an_ref), nw) * keep_next).astype(BF16)
    acc_ref[...] = x
    fc = FF_CHUNK
    n_chunks = D_FF // fc
    rb = FFN_ROW_BLOCK
    win = rb + 16

    ext = tm + 2 * HALO
    up_rows = [(0, 144), (144, 272), (272, 400), (400, ext)]
    down_rows = [(k * (tm // 4), (k + 1) * (tm // 4)) for k in range(4)]

    def up_piece(c, part, lo, hi):
        col0 = part * D_FF + c * fc
        u_ref[c % 2, part, lo:hi, :] = jnp.dot(h_ref[lo:hi, :], wu_ref[:, col0:col0 + fc],
                                               preferred_element_type=F32)

    def down_piece(c, lo, hi):
        acc_ref[lo:hi, :] += jnp.dot(act_ref[c % 2, lo:hi, :], wd_ref[c * fc:(c + 1) * fc, :],
                                     preferred_element_type=F32)

    def conv(slot, part, r, col0):
        uw = u_ref[slot, part, HALO - 8 + r * rb:HALO + 8 + (r + 1) * rb, :]
        wc = wc_ref[:, col0:col0 + fc]
        u_prev = pltpu.roll(uw, 1, axis=0)[8:8 + rb, :]
        u_next = pltpu.roll(uw, win - 1, axis=0)[8:8 + rb, :]
        return (u_prev * wc[0:1, :] + uw[8:8 + rb, :] * wc[1:2, :] + u_next * wc[2:3, :]
                + bc_ref[:, col0:col0 + fc])

    def elem_block(c, r):
        val = conv(c % 2, 0, r, c * fc)
        gate = conv(c % 2, 1, r, D_FF + c * fc)
        act_ref[c % 2, r * rb:(r + 1) * rb, :] = (_silu(gate) * val).astype(BF16)

    def stage(c):
        pieces = []
        if c + 1 < n_chunks:
            for lo, hi in up_rows:
                pieces.append(functools.partial(up_piece, c + 1, 0, lo, hi))
                pieces.append(functools.partial(up_piece, c + 1, 1, lo, hi))
        if c > 0:
            for lo, hi in down_rows:
                pieces.append(functools.partial(down_piece, c - 1, lo, hi))
        n_blocks = tm // rb
        done = 0
        for k, piece in enumerate(pieces):
            piece()
            upto = (k + 1) * n_blocks // len(pieces)
            for r in range(done, upto):
                elem_block(c, r)
            done = upto

    for lo, hi in up_rows:
        up_piece(0, 0, lo, hi)
        up_piece(0, 1, lo, hi)
    for c in range(n_chunks):
        stage(c)
    for lo, hi in down_rows:
        down_piece(n_chunks - 1, lo, hi)
    o_ref[...] = acc_ref[...]


def _mix_ffn(x, a, wo, nw, wu, wc, bc, wd, layer, mixer, tm, seq):
    t, d = x.shape
    ka = a.shape[1]
    tiles_per_seq = seq // tm
    hb = tm // HALO
    n_hblk = t // HALO
    main = lambda i: (i, 0)
    prev = lambda i: (jnp.maximum(i * hb - 1, 0), 0)
    nxt = lambda i: (jnp.minimum((i + 1) * hb, n_hblk - 1), 0)
    const = lambda i: (0, 0)
    return pl.pallas_call(
        functools.partial(_ffn_kernel, tm=tm, tiles_per_seq=tiles_per_seq),
        grid=(t // tm,),
        in_specs=[
            pl.BlockSpec((tm, d), main),
            pl.BlockSpec((HALO, d), prev),
            pl.BlockSpec((HALO, d), nxt),
            pl.BlockSpec((tm, ka), main),
            pl.BlockSpec((HALO, ka), prev),
            pl.BlockSpec((HALO, ka), nxt),
            pl.BlockSpec((None, ka, d), lambda i: (mixer, 0, 0), pipeline_mode=pl.Buffered(1)),
            pl.BlockSpec((1, d), const),
            pl.BlockSpec((None, d, 2 * D_FF), lambda i: (layer, 0, 0), pipeline_mode=pl.Buffered(1)),
            pl.BlockSpec((None, 3, 2 * D_FF), lambda i: (layer, 0, 0)),
            pl.BlockSpec((None, 1, 2 * D_FF), lambda i: (layer, 0, 0)),
            pl.BlockSpec((None, D_FF, d), lambda i: (layer, 0, 0), pipeline_mode=pl.Buffered(1)),
        ],
        out_specs=pl.BlockSpec((tm, d), main),
        out_shape=jax.ShapeDtypeStruct((t, d), F32),
        scratch_shapes=[
            pltpu.VMEM((tm + 2 * HALO, d), BF16),
            pltpu.VMEM((2, 2, tm + 2 * HALO, FF_CHUNK), F32),
            pltpu.VMEM((2, tm, FF_CHUNK), BF16),
            pltpu.VMEM((tm, d), F32),
        ],
        compiler_params=_cparams(("parallel",)),
        name="mix_conv_ffn",
    )(x, x, x, a, a, a, wo, nw, wu, wc, bc, wd)


def _gla_gate_weights(w_f, b_f, w_b, b_b):
    wf = w_f.reshape(GLA_RANK, GLA_HEADS, GLA_DK).transpose(1, 0, 2)
    wb = w_b.reshape(GLA_RANK, GLA_HEADS, GLA_DK).transpose(1, 0, 2)
    z = jnp.zeros_like(wf)
    top = jnp.concatenate([wf, z], axis=-1)
    mid = jnp.concatenate([z, wb], axis=-1)
    pad = jnp.zeros((GLA_HEADS, 128 - 2 * GLA_RANK, 2 * GLA_DK), w_f.dtype)
    wg = jnp.concatenate([top, mid, pad], axis=1).astype(BF16)
    bg = jnp.concatenate([b_f.reshape(GLA_HEADS, 1, GLA_DK), b_b.reshape(GLA_HEADS, 1, GLA_DK)], axis=-1)
    return wg, bg.astype(F32)


def _rope_tables(seq):
    rows = seq // GRID_W
    row_idx = jnp.repeat(jnp.arange(rows, dtype=F32), GRID_W)
    col_idx = jnp.tile(jnp.arange(GRID_W, dtype=F32), rows)
    inv_freq = ROPE_THETA ** (-jnp.arange(ROPE_PAIRS_PER_AXIS, dtype=F32) / ROPE_PAIRS_PER_AXIS)
    ang = jnp.concatenate([row_idx[:, None] * inv_freq, col_idx[:, None] * inv_freq], axis=-1)
    cos = jnp.cos(ang)
    sin = jnp.sin(ang)
    return jnp.concatenate([cos, cos], axis=-1), jnp.concatenate([-sin, sin], axis=-1)


def kernel(x, norm_mix, norm_ffn, gla_w_in, gla_w_gate_up_f, gla_b_gate_f, gla_w_gate_up_b, gla_b_gate_b,
           gla_norm, gla_w_out, attn_w_qkv, attn_q_norm, attn_k_norm, attn_w_out,
           ffn_w_up, ffn_w_conv, ffn_b_conv, ffn_w_down):
    batch, seq, d = x.shape
    depth = norm_mix.shape[0]
    t = batch * seq
    tm = 512
    cosf, sinf = _rope_tables(seq)
    q_scale = (HEAD_DIM ** -0.5) * math.log2(math.e)
    xf = x.reshape(t, d)
    w_in_all = jnp.pad(gla_w_in, ((0, 0), (0, 0), (0, GLA_PROJ_COLS - gla_w_in.shape[-1]))).astype(BF16)
    w_qkv_all = attn_w_qkv.astype(BF16)
    gla_wo_all = gla_w_out.astype(BF16)
    attn_wo_all = attn_w_out.astype(BF16)
    wu_all = ffn_w_up.astype(BF16)
    wd_all = ffn_w_down.astype(BF16)
    bc_all = ffn_b_conv.reshape(depth, 1, 2 * D_FF)
    for i in range(depth):
        j = i // 2
        nw = norm_mix[i].reshape(1, d)
        if i % 2 == 0:
            wg, bg = _gla_gate_weights(gla_w_gate_up_f[j], gla_b_gate_f[j], gla_w_gate_up_b[j], gla_b_gate_b[j])
            proj = _norm_proj(xf, nw, w_in_all, j, tm)
            a = _gla(proj, wg, bg, gla_norm[j].reshape(1, GLA_DV), batch, seq)
            wo_all = gla_wo_all
        else:
            qkv = _qkv_proj(xf, nw, w_qkv_all, j, attn_q_norm[j].reshape(1, HEAD_DIM),
                            attn_k_norm[j].reshape(1, HEAD_DIM), cosf, sinf, tm, seq, q_scale)
            a = _attention(qkv, batch, seq, tq=256, tk=min(1024, seq // 2))
            wo_all = attn_wo_all
        xf = _mix_ffn(xf, a, wo_all, norm_ffn[i].reshape(1, d), wu_all, ffn_w_conv, bc_all, wd_all,
                      i, j, tm, seq)
    return xf.reshape(batch, seq, d)
```

```python
import functools
import math

import jax
import jax.numpy as jnp
from jax import lax
from jax.experimental import pallas as pl
from jax.experimental.pallas import tpu as pltpu

F32 = jnp.float32
BF16 = jnp.bfloat16

D_MODEL = 1024
GRID_W = 64
NORM_EPS = 1e-6

GLA_HEADS = 4
GLA_KEY_DIM = 512
GLA_VAL_DIM = 1024
GLA_DK = 128
GLA_DV = 256
GLA_RANK = 16
GLA_GATE_NORMALIZER = 16.0
GLA_CHUNK = 64
GLA_GROUP = 4
GLA_GROUPS_PER_STEP = 4
GLA_MAIN_COLS = 2 * GLA_KEY_DIM + 2 * GLA_VAL_DIM
GLA_PROJ_COLS = GLA_MAIN_COLS + 128

HEAD_DIM = 128
Q_HEADS = 8
KV_HEADS = 2
GROUP = Q_HEADS // KV_HEADS
QKV_DIM = (Q_HEADS + 2 * KV_HEADS) * HEAD_DIM
ROPE_THETA = 10000.0
ROPE_PAIRS_PER_AXIS = HEAD_DIM // 4

D_FF = 2816
FF_CHUNK = 256
FFN_ROW_BLOCK = 128
HALO = 16

VMEM_LIMIT = 56 * 1024 * 1024


def _cparams(sem):
    return pltpu.CompilerParams(dimension_semantics=sem, vmem_limit_bytes=VMEM_LIMIT)


def _rmsnorm_rows(x, w):
    ms = jnp.mean(x * x, axis=-1, keepdims=True)
    return x * lax.rsqrt(ms + NORM_EPS) * w


def _norm_proj_kernel(x_ref, nw_ref, w_ref, o_ref, *, col_chunks):
    h = _rmsnorm_rows(x_ref[...], nw_ref[...]).astype(BF16)
    for lo, hi in col_chunks:
        o_ref[:, lo:hi] = jnp.dot(h, w_ref[:, lo:hi], preferred_element_type=F32).astype(o_ref.dtype)


def _norm_proj(x, nw, w, layer, tm):
    t, d = x.shape
    n = w.shape[-1]
    chunks = []
    lo = 0
    while lo < n:
        hi = min(lo + 1024, n)
        chunks.append((lo, hi))
        lo = hi
    return pl.pallas_call(
        functools.partial(_norm_proj_kernel, col_chunks=tuple(chunks)),
        grid=(t // tm,),
        in_specs=[
            pl.BlockSpec((tm, d), lambda i: (i, 0)),
            pl.BlockSpec((1, d), lambda i: (0, 0)),
            pl.BlockSpec((None, d, n), lambda i: (layer, 0, 0), pipeline_mode=pl.Buffered(1)),
        ],
        out_specs=pl.BlockSpec((tm, n), lambda i: (i, 0)),
        out_shape=jax.ShapeDtypeStruct((t, n), BF16),
        compiler_params=_cparams(("parallel",)),
        name="gla_in_proj",
    )(x, nw, w)


def _qkv_proj_kernel(x_ref, nw_ref, w_ref, qn_ref, kn_ref, cos_ref, sin_ref, o_ref, *, q_scale):
    h = _rmsnorm_rows(x_ref[...], nw_ref[...]).astype(BF16)
    cosf = cos_ref[...]
    sinf = sin_ref[...]
    n_rot = Q_HEADS + KV_HEADS
    n_heads = QKV_DIM // HEAD_DIM
    ys = [jnp.dot(h, w_ref[:, c * 512:(c + 1) * 512], preferred_element_type=F32) for c in range(QKV_DIM // 512)]
    heads = [ys[hd // 4][:, (hd % 4) * HEAD_DIM:(hd % 4 + 1) * HEAD_DIM] for hd in range(n_heads)]
    ms = [jnp.mean(heads[hd] * heads[hd], axis=-1, keepdims=True) for hd in range(n_rot)]
    inv = [lax.rsqrt(m + NORM_EPS) for m in ms]
    yn = [heads[hd] * inv[hd] * (qn_ref[...] if hd < Q_HEADS else kn_ref[...]) for hd in range(n_rot)]
    rot = [pltpu.roll(y, HEAD_DIM // 2, axis=1) for y in yn]
    for hd in range(n_heads):
        if hd < n_rot:
            yh = yn[hd] * cosf + rot[hd] * sinf
            if hd < Q_HEADS:
                yh = yh * q_scale
        else:
            yh = heads[hd]
        o_ref[:, hd * HEAD_DIM:(hd + 1) * HEAD_DIM] = yh.astype(o_ref.dtype)


def _qkv_proj(x, nw, w, layer, qn, kn, cosf, sinf, tm, seq, q_scale):
    t, d = x.shape
    tiles_per_seq = seq // tm
    return pl.pallas_call(
        functools.partial(_qkv_proj_kernel, q_scale=q_scale),
        grid=(t // tm,),
        in_specs=[
            pl.BlockSpec((tm, d), lambda i: (i, 0)),
            pl.BlockSpec((1, d), lambda i: (0, 0)),
            pl.BlockSpec((None, d, QKV_DIM), lambda i: (layer, 0, 0), pipeline_mode=pl.Buffered(1)),
            pl.BlockSpec((1, HEAD_DIM), lambda i: (0, 0)),
            pl.BlockSpec((1, HEAD_DIM), lambda i: (0, 0)),
            pl.BlockSpec((tm, HEAD_DIM), lambda i: (i % tiles_per_seq, 0)),
            pl.BlockSpec((tm, HEAD_DIM), lambda i: (i % tiles_per_seq, 0)),
        ],
        out_specs=pl.BlockSpec((tm, QKV_DIM), lambda i: (i, 0)),
        out_shape=jax.ShapeDtypeStruct((t, QKV_DIM), BF16),
        compiler_params=_cparams(("parallel",)),
        name="attn_qkv_proj",
    )(x, nw, w, qn, kn, cosf, sinf)


def _log_sigmoid(x):
    return jnp.minimum(x, 0.0) - jnp.log(1.0 + jnp.exp(-jnp.abs(x)))


def _silu(x):
    return x / (1.0 + jnp.exp(-x))


def _gla_kernel(q_ref, k_ref, v_ref, g_ref, r_ref, wg_ref, bg_ref, nw_ref, o_ref,
                la_ref, oacc_ref, sf_ref, sb_ref, *, n_chunks):
    c_len = GLA_CHUNK
    n_sub = GLA_GROUP
    g_len = n_sub * c_len
    n_groups = n_chunks // n_sub
    n_par = GLA_GROUPS_PER_STEP
    logits = jnp.dot(r_ref[...], wg_ref[0], preferred_element_type=F32) + bg_ref[0]
    la_ref[...] = _log_sigmoid(logits) * (1.0 / GLA_GATE_NORMALIZER)
    sf_ref[...] = jnp.zeros_like(sf_ref)
    sb_ref[...] = jnp.zeros_like(sb_ref)

    row = lax.broadcasted_iota(jnp.int32, (g_len, g_len), 0)
    col = lax.broadcasted_iota(jnp.int32, (g_len, g_len), 1)
    same_chunk = (row // c_len) == (col // c_len)
    cum_prefix = jnp.where(same_chunk & (row >= col), 1.0, 0.0).astype(BF16)
    q_scale = GLA_DK ** -0.5
    nw = nw_ref[...]
    trans_b = (((1,), (1,)), ((), ()))
    trans_a = (((0,), (0,)), ((), ()))

    def step(i, final):
        work = []
        for forward in (True, False):
            for s in range(n_par):
                gi = i * n_par + s
                g = gi if forward else n_groups - 1 - gi
                base = pl.multiple_of(g * g_len, g_len)
                if forward:
                    lanes, keep = slice(0, GLA_DK), row >= col
                else:
                    lanes, keep = slice(GLA_DK, 2 * GLA_DK), col > row
                work.append(dict(forward=forward, base=base, rows=pl.ds(base, g_len), lanes=lanes,
                                 mask=same_chunk & keep))

        n_half = len(work) // 2
        for fw, bw in zip(work[:n_half], work[n_half:]):
            la = jnp.concatenate([la_ref[fw["rows"], fw["lanes"]], la_ref[bw["rows"], bw["lanes"]]], axis=1)
            la_hi = la.astype(BF16)
            la_lo = (la - la_hi.astype(F32)).astype(BF16)
            pre = (jnp.dot(cum_prefix, la_hi, preferred_element_type=F32)
                   + jnp.dot(cum_prefix, la_lo, preferred_element_type=F32))
            fw["b"] = pre[:, :GLA_DK]
            pre_b = pre[:, GLA_DK:]
            tot = jnp.concatenate(
                [jnp.broadcast_to(pre_b[u * c_len + c_len - 1:u * c_len + c_len, :], (c_len, GLA_DK))
                 for u in range(n_sub)], axis=0)
            bw["b"] = tot - pre_b + la[:, GLA_DK:]

        for d in work:
            b = d["b"]
            ends = []
            for u in range(n_sub):
                r = u * c_len + (c_len - 1 if d["forward"] else 0)
                ends.append(b[r:r + 1, :])
            d["decay"] = [jnp.exp(e) for e in ends]
            dec_rows = jnp.concatenate([jnp.broadcast_to(e, (c_len, GLA_DK)) for e in d["decay"]], axis=0)
            q = q_ref[d["rows"], :].astype(F32)
            k = k_ref[d["rows"], :].astype(F32)
            d["v"] = v_ref[d["rows"], :]
            d["q_dec"] = (q * (jnp.exp(b) * q_scale)).astype(BF16)
            k_inv = k * jnp.exp(-b)
            d["k_out"] = (k_inv * dec_rows).astype(BF16)
            att = lax.dot_general(d["q_dec"], k_inv.astype(BF16), trans_b, preferred_element_type=F32)
            d["att"] = jnp.where(d["mask"], att, 0.0).astype(BF16)

        for d in work:
            d["o"] = jnp.dot(d["att"], d["v"], preferred_element_type=F32)
            d["kv"] = []
            for u in range(n_sub):
                sl = slice(u * c_len, (u + 1) * c_len)
                d["kv"].append(lax.dot_general(d["k_out"][sl, :], d["v"][sl, :], trans_a,
                                               preferred_element_type=F32))

        for forward in (True, False):
            st_ref = sf_ref if forward else sb_ref
            state = st_ref[...]
            for d in (w for w in work if w["forward"] == forward):
                for u in (range(n_sub) if forward else range(n_sub - 1, -1, -1)):
                    sl = slice(u * c_len, (u + 1) * c_len)
                    o_u = d["o"][sl, :] + jnp.dot(d["q_dec"][sl, :], state.astype(BF16),
                                                  preferred_element_type=F32)
                    dec_col = jnp.broadcast_to(d["decay"][u], (GLA_DK, GLA_DK)).T
                    state = state * jnp.concatenate([dec_col] * (GLA_DV // GLA_DK), axis=1) + d["kv"][u]
                    rows_u = pl.ds(pl.multiple_of(d["base"] + u * c_len, c_len), c_len)
                    if final:
                        o_sum = o_u + oacc_ref[rows_u, :]
                        g = g_ref[rows_u, :].astype(F32)
                        o_ref[rows_u, :] = (_rmsnorm_rows(o_sum, nw) * _silu(g)).astype(o_ref.dtype)
                    else:
                        oacc_ref[rows_u, :] = o_u
            st_ref[...] = state

    n_steps = n_groups // n_par
    half = n_steps // 2

    def first_half(i, carry):
        step(i, False)
        return carry

    def second_half(i, carry):
        step(i, True)
        return carry

    lax.fori_loop(0, half, first_half, 0)
    lax.fori_loop(half, n_steps, second_half, 0)


def _gla(proj, wg, bg, nw, batch, seq):
    t = batch * seq
    n_chunks = seq // GLA_CHUNK
    assert n_chunks % (2 * GLA_GROUP * GLA_GROUPS_PER_STEP) == 0
    kq = GLA_KEY_DIM // GLA_DK
    return pl.pallas_call(
        functools.partial(_gla_kernel, n_chunks=n_chunks),
        grid=(batch, GLA_HEADS),
        in_specs=[
            pl.BlockSpec((seq, GLA_DK), lambda b, h: (b, h)),
            pl.BlockSpec((seq, GLA_DK), lambda b, h: (b, kq + h)),
            pl.BlockSpec((seq, GLA_DV), lambda b, h: (b, (2 * GLA_KEY_DIM) // GLA_DV + h)),
            pl.BlockSpec((seq, GLA_DV), lambda b, h: (b, (2 * GLA_KEY_DIM + GLA_VAL_DIM) // GLA_DV + h)),
            pl.BlockSpec((seq, 128), lambda b, h: (b, GLA_MAIN_COLS // 128)),
            pl.BlockSpec((1, 128, 2 * GLA_DK), lambda b, h: (h, 0, 0)),
            pl.BlockSpec((1, 1, 2 * GLA_DK), lambda b, h: (h, 0, 0)),
            pl.BlockSpec((1, GLA_DV), lambda b, h: (0, 0)),
        ],
        out_specs=pl.BlockSpec((seq, GLA_DV), lambda b, h: (b, h)),
        out_shape=jax.ShapeDtypeStruct((t, GLA_VAL_DIM), BF16),
        scratch_shapes=[
            pltpu.VMEM((seq, 2 * GLA_DK), F32),
            pltpu.VMEM((seq, GLA_DV), F32),
            pltpu.VMEM((GLA_DK, GLA_DV), F32),
            pltpu.VMEM((GLA_DK, GLA_DV), F32),
        ],
        compiler_params=_cparams(("parallel", "parallel")),
        name="gla_scan",
    )(proj, proj, proj, proj, proj, wg, bg, nw)


def _attn_kernel(q_ref, k_ref, v_ref, o_ref, s_ref, m_ref, acc_ref, va_ref, *, tq, tk, n_q, n_kv):
    assert n_kv % 2 == 0

    def scores(tile, blk, slot):
        rows = pl.ds(pl.multiple_of(tile * tq, tq), tq)
        kb = k_ref[blk * tk:(blk + 1) * tk, :]
        for g in range(GROUP):
            qg = q_ref[rows, g * HEAD_DIM:(g + 1) * HEAD_DIM]
            s_ref[slot, g * tq:(g + 1) * tq, :] = lax.dot_general(
                qg, kb, (((1,), (1,)), ((), ())), preferred_element_type=F32)

    def softmax_pv(tile, blk, slot):
        vb = va_ref[blk * tk:(blk + 1) * tk, :]
        for g in range(GROUP):
            gr = slice(g * tq, (g + 1) * tq)
            s = s_ref[slot, gr, :]
            m_blk = jnp.max(s, axis=-1, keepdims=True)
            if blk == 0:
                m_new = m_blk
            else:
                m_old = m_ref[gr, :]
                m_new = jnp.maximum(m_old, m_blk)
                alpha = jnp.exp2(m_old - m_new)
            p = jnp.exp2(s - m_new).astype(BF16)
            pv = jnp.dot(p, vb, preferred_element_type=F32)
            acc_new = pv if blk == 0 else alpha * acc_ref[gr, :] + pv
            if blk == n_kv - 1:
                rows = pl.ds(pl.multiple_of(tile * tq, tq), tq)
                o_ref[rows, g * HEAD_DIM:(g + 1) * HEAD_DIM] = (
                    acc_new[:, :HEAD_DIM] / acc_new[:, HEAD_DIM:]).astype(o_ref.dtype)
            else:
                m_ref[gr, :] = m_new
                acc_ref[gr, :] = acc_new

    va_ref[:, :HEAD_DIM] = v_ref[...]
    va_ref[:, HEAD_DIM:] = jnp.ones((va_ref.shape[0], HEAD_DIM), BF16)
    scores(0, 0, 0)

    def tile_body(qi, carry):
        for j in range(n_kv):
            if j + 1 < n_kv:
                scores(qi, j + 1, (j + 1) % 2)
            else:
                scores(jnp.minimum(qi + 1, n_q - 1), 0, 0)
            softmax_pv(qi, j, j % 2)
        return carry

    lax.fori_loop(0, n_q, tile_body, 0)


def _attention(qkv, batch, seq, tq, tk):
    t = batch * seq
    gw = GROUP * HEAD_DIM
    rows = GROUP * tq
    return pl.pallas_call(
        functools.partial(_attn_kernel, tq=tq, tk=tk, n_q=seq // tq, n_kv=seq // tk),
        grid=(batch, KV_HEADS),
        in_specs=[
            pl.BlockSpec((seq, gw), lambda b, h: (b, h)),
            pl.BlockSpec((seq, HEAD_DIM), lambda b, h: (b, Q_HEADS + h)),
            pl.BlockSpec((seq, HEAD_DIM), lambda b, h: (b, Q_HEADS + KV_HEADS + h)),
        ],
        out_specs=pl.BlockSpec((seq, gw), lambda b, h: (b, h)),
        out_shape=jax.ShapeDtypeStruct((t, Q_HEADS * HEAD_DIM), BF16),
        scratch_shapes=[
            pltpu.VMEM((2, rows, tk), F32),
            pltpu.VMEM((rows, 1), F32),
            pltpu.VMEM((rows, 2 * HEAD_DIM), F32),
            pltpu.VMEM((seq, 2 * HEAD_DIM), BF16),
        ],
        compiler_params=_cparams(("parallel", "parallel")),
        name="gqa_attention",
    )(qkv, qkv, qkv)


def _ffn_kernel(x_ref, xp_ref, xn_ref, a_ref, ap_ref, an_ref, wo_ref, nw_ref, wu_ref, wc_ref, bc_ref, wd_ref,
                o_ref, h_ref, u_ref, act_ref, acc_ref, *, tm, tiles_per_seq):
    i = pl.program_id(0)
    nw = nw_ref[...]
    wo = wo_ref[...]

    def mixed(xr, ar):
        return xr[...] + jnp.dot(ar[...], wo, preferred_element_type=F32)

    x = mixed(x_ref, a_ref)
    keep_prev = jnp.where(i % tiles_per_seq == 0, 0.0, 1.0)
    keep_next = jnp.where(i % tiles_per_seq == tiles_per_seq - 1, 0.0, 1.0)
    h_ref[0:HALO, :] = (_rmsnorm_rows(mixed(xp_ref, ap_ref), nw) * keep_prev).astype(BF16)
    h_ref[HALO:HALO + tm, :] = _rmsnorm_rows(x, nw).astype(BF16)
    h_ref[HALO + tm:2 * HALO + tm, :] = (_rmsnorm_rows(mixed(xn_ref, an_ref), nw) * keep_next).astype(BF16)
    acc_ref[...] = x
    fc = FF_CHUNK
    n_chunks = D_FF // fc
    rb = FFN_ROW_BLOCK
    win = rb + 16

    ext = tm + 2 * HALO
    up_rows = [(0, 144), (144, 272), (272, 400), (400, ext)]
    down_rows = [(k * (tm // 4), (k + 1) * (tm // 4)) for k in range(4)]

    def up_piece(c, part, lo, hi):
        col0 = part * D_FF + c * fc
        u_ref[c % 2, part, lo:hi, :] = jnp.dot(h_ref[lo:hi, :], wu_ref[:, col0:col0 + fc],
                                               preferred_element_type=F32)

    def down_piece(c, lo, hi):
        acc_ref[lo:hi, :] += jnp.dot(act_ref[c % 2, lo:hi, :], wd_ref[c * fc:(c + 1) * fc, :],
                                     preferred_element_type=F32)

    def conv(slot, part, r, col0):
        uw = u_ref[slot, part, HALO - 8 + r * rb:HALO + 8 + (r + 1) * rb, :]
        wc = wc_ref[:, col0:col0 + fc]
        u_prev = pltpu.roll(uw, 1, axis=0)[8:8 + rb, :]
        u_next = pltpu.roll(uw, win - 1, axis=0)[8:8 + rb, :]
        return (u_prev * wc[0:1, :] + uw[8:8 + rb, :] * wc[1:2, :] + u_next * wc[2:3, :]
                + bc_ref[:, col0:col0 + fc])

    def elem_block(c, r):
        val = conv(c % 2, 0, r, c * fc)
        gate = conv(c % 2, 1, r, D_FF + c * fc)
        act_ref[c % 2, r * rb:(r + 1) * rb, :] = (_silu(gate) * val).astype(BF16)

    def stage(c):
        pieces = []
        if c + 1 < n_chunks:
            for lo, hi in up_rows:
                pieces.append(functools.partial(up_piece, c + 1, 0, lo, hi))
                pieces.append(functools.partial(up_piece, c + 1, 1, lo, hi))
        if c > 0:
            for lo, hi in down_rows:
                pieces.append(functools.partial(down_piece, c - 1, lo, hi))
        n_blocks = tm // rb
        done = 0
        for k, piece in enumerate(pieces):
            piece()
            upto = (k + 1) * n_blocks // len(pieces)
            for r in range(done, upto):
                elem_block(c, r)
            done = upto

    for lo, hi in up_rows:
        up_piece(0, 0, lo, hi)
        up_piece(0, 1, lo, hi)
    for c in range(n_chunks):
        stage(c)
    for lo, hi in down_rows:
        down_piece(n_chunks - 1, lo, hi)
    o_ref[...] = acc_ref[...]


def _mix_ffn(x, a, wo, nw, wu, wc, bc, wd, layer, mixer, tm, seq):
    t, d = x.shape
    ka = a.shape[1]
    tiles_per_seq = seq // tm
    hb = tm // HALO
    n_hblk = t // HALO
    main = lambda i: (i, 0)
    prev = lambda i: (jnp.maximum(i * hb - 1, 0), 0)
    nxt = lambda i: (jnp.minimum((i + 1) * hb, n_hblk - 1), 0)
    const = lambda i: (0, 0)
    return pl.pallas_call(
        functools.partial(_ffn_kernel, tm=tm, tiles_per_seq=tiles_per_seq),
        grid=(t // tm,),
        in_specs=[
            pl.BlockSpec((tm, d), main),
            pl.BlockSpec((HALO, d), prev),
            pl.BlockSpec((HALO, d), nxt),
            pl.BlockSpec((tm, ka), main),
            pl.BlockSpec((HALO, ka), prev),
            pl.BlockSpec((HALO, ka), nxt),
            pl.BlockSpec((None, ka, d), lambda i: (mixer, 0, 0), pipeline_mode=pl.Buffered(1)),
            pl.BlockSpec((1, d), const),
            pl.BlockSpec((None, d, 2 * D_FF), lambda i: (layer, 0, 0), pipeline_mode=pl.Buffered(1)),
            pl.BlockSpec((None, 3, 2 * D_FF), lambda i: (layer, 0, 0)),
            pl.BlockSpec((None, 1, 2 * D_FF), lambda i: (layer, 0, 0)),
            pl.BlockSpec((None, D_FF, d), lambda i: (layer, 0, 0), pipeline_mode=pl.Buffered(1)),
        ],
        out_specs=pl.BlockSpec((tm, d), main),
        out_shape=jax.ShapeDtypeStruct((t, d), F32),
        scratch_shapes=[
            pltpu.VMEM((tm + 2 * HALO, d), BF16),
            pltpu.VMEM((2, 2, tm + 2 * HALO, FF_CHUNK), F32),
            pltpu.VMEM((2, tm, FF_CHUNK), BF16),
            pltpu.VMEM((tm, d), F32),
        ],
        compiler_params=_cparams(("parallel",)),
        name="mix_conv_ffn",
    )(x, x, x, a, a, a, wo, nw, wu, wc, bc, wd)


def _gla_gate_weights(w_f, b_f, w_b, b_b):
    wf = w_f.reshape(GLA_RANK, GLA_HEADS, GLA_DK).transpose(1, 0, 2)
    wb = w_b.reshape(GLA_RANK, GLA_HEADS, GLA_DK).transpose(1, 0, 2)
    z = jnp.zeros_like(wf)
    top = jnp.concatenate([wf, z], axis=-1)
    mid = jnp.concatenate([z, wb], axis=-1)
    pad = jnp.zeros((GLA_HEADS, 128 - 2 * GLA_RANK, 2 * GLA_DK), w_f.dtype)
    wg = jnp.concatenate([top, mid, pad], axis=1).astype(BF16)
    bg = jnp.concatenate([b_f.reshape(GLA_HEADS, 1, GLA_DK), b_b.reshape(GLA_HEADS, 1, GLA_DK)], axis=-1)
    return wg, bg.astype(F32)


def _rope_tables(seq):
    rows = seq // GRID_W
    row_idx = jnp.repeat(jnp.arange(rows, dtype=F32), GRID_W)
    col_idx = jnp.tile(jnp.arange(GRID_W, dtype=F32), rows)
    inv_freq = ROPE_THETA ** (-jnp.arange(ROPE_PAIRS_PER_AXIS, dtype=F32) / ROPE_PAIRS_PER_AXIS)
    ang = jnp.concatenate([row_idx[:, None] * inv_freq, col_idx[:, None] * inv_freq], axis=-1)
    cos = jnp.cos(ang)
    sin = jnp.sin(ang)
    return jnp.concatenate([cos, cos], axis=-1), jnp.concatenate([-sin, sin], axis=-1)


def kernel(x, norm_mix, norm_ffn, gla_w_in, gla_w_gate_up_f, gla_b_gate_f, gla_w_gate_up_b, gla_b_gate_b,
           gla_norm, gla_w_out, attn_w_qkv, attn_q_norm, attn_k_norm, attn_w_out,
           ffn_w_up, ffn_w_conv, ffn_b_conv, ffn_w_down):
    batch, seq, d = x.shape
    depth = norm_mix.shape[0]
    t = batch * seq
    tm = 512
    cosf, sinf = _rope_tables(seq)
    q_scale = (HEAD_DIM ** -0.5) * math.log2(math.e)
    xf = x.reshape(t, d)
    w_in_all = jnp.pad(gla_w_in, ((0, 0), (0, 0), (0, GLA_PROJ_COLS - gla_w_in.shape[-1]))).astype(BF16)
    w_qkv_all = attn_w_qkv.astype(BF16)
    gla_wo_all = gla_w_out.astype(BF16)
    attn_wo_all = attn_w_out.astype(BF16)
    wu_all = ffn_w_up.astype(BF16)
    wd_all = ffn_w_down.astype(BF16)
    bc_all = ffn_b_conv.reshape(depth, 1, 2 * D_FF)
    for i in range(depth):
        j = i // 2
        nw = norm_mix[i].reshape(1, d)
        if i % 2 == 0:
            wg, bg = _gla_gate_weights(gla_w_gate_up_f[j], gla_b_gate_f[j], gla_w_gate_up_b[j], gla_b_gate_b[j])
            proj = _norm_proj(xf, nw, w_in_all, j, tm)
            a = _gla(proj, wg, bg, gla_norm[j].reshape(1, GLA_DV), batch, seq)
            wo_all = gla_wo_all
        else:
            qkv = _qkv_proj(xf, nw, w_qkv_all, j, attn_q_norm[j].reshape(1, HEAD_DIM),
                            attn_k_norm[j].reshape(1, HEAD_DIM), cosf, sinf, tm, seq, q_scale)
            a = _attention(qkv, batch, seq, tq=256, tk=min(2048, seq // 2))
            wo_all = attn_wo_all
        xf = _mix_ffn(xf, a, wo_all, norm_ffn[i].reshape(1, d), wu_all, ffn_w_conv, bc_all, wd_all,
                      i, j, tm, seq)
    return xf.reshape(batch, seq, d)
```

```python
import functools
import math

import jax
import jax.numpy as jnp
from jax import lax
from jax.experimental import pallas as pl
from jax.experimental.pallas import tpu as pltpu

F32 = jnp.float32
BF16 = jnp.bfloat16

D_MODEL = 1024
GRID_W = 64
NORM_EPS = 1e-6

GLA_HEADS = 4
GLA_KEY_DIM = 512
GLA_VAL_DIM = 1024
GLA_DK = 128
GLA_DV = 256
GLA_RANK = 16
GLA_GATE_NORMALIZER = 16.0
GLA_CHUNK = 64
GLA_GROUP = 4
GLA_GROUPS_PER_STEP = 8
GLA_MAIN_COLS = 2 * GLA_KEY_DIM + 2 * GLA_VAL_DIM
GLA_PROJ_COLS = GLA_MAIN_COLS + 128

HEAD_DIM = 128
Q_HEADS = 8
KV_HEADS = 2
GROUP = Q_HEADS // KV_HEADS
QKV_DIM = (Q_HEADS + 2 * KV_HEADS) * HEAD_DIM
ROPE_THETA = 10000.0
ROPE_PAIRS_PER_AXIS = HEAD_DIM // 4

D_FF = 2816
FF_CHUNK = 256
FFN_ROW_BLOCK = 128
HALO = 16

VMEM_LIMIT = 56 * 1024 * 1024


def _cparams(sem):
    return pltpu.CompilerParams(dimension_semantics=sem, vmem_limit_bytes=VMEM_LIMIT)


def _rmsnorm_rows(x, w):
    ms = jnp.mean(x * x, axis=-1, keepdims=True)
    return x * lax.rsqrt(ms + NORM_EPS) * w


def _norm_proj_kernel(x_ref, nw_ref, w_ref, o_ref, *, col_chunks):
    h = _rmsnorm_rows(x_ref[...], nw_ref[...]).astype(BF16)
    for lo, hi in col_chunks:
        o_ref[:, lo:hi] = jnp.dot(h, w_ref[:, lo:hi], preferred_element_type=F32).astype(o_ref.dtype)


def _norm_proj(x, nw, w, layer, tm):
    t, d = x.shape
    n = w.shape[-1]
    chunks = []
    lo = 0
    while lo < n:
        hi = min(lo + 1024, n)
        chunks.append((lo, hi))
        lo = hi
    return pl.pallas_call(
        functools.partial(_norm_proj_kernel, col_chunks=tuple(chunks)),
        grid=(t // tm,),
        in_specs=[
            pl.BlockSpec((tm, d), lambda i: (i, 0)),
            pl.BlockSpec((1, d), lambda i: (0, 0)),
            pl.BlockSpec((None, d, n), lambda i: (layer, 0, 0), pipeline_mode=pl.Buffered(1)),
        ],
        out_specs=pl.BlockSpec((tm, n), lambda i: (i, 0)),
        out_shape=jax.ShapeDtypeStruct((t, n), BF16),
        compiler_params=_cparams(("parallel",)),
        name="gla_in_proj",
    )(x, nw, w)


def _qkv_proj_kernel(x_ref, nw_ref, w_ref, qn_ref, kn_ref, cos_ref, sin_ref, o_ref, *, q_scale):
    h = _rmsnorm_rows(x_ref[...], nw_ref[...]).astype(BF16)
    cosf = cos_ref[...]
    sinf = sin_ref[...]
    n_rot = Q_HEADS + KV_HEADS
    n_heads = QKV_DIM // HEAD_DIM
    ys = [jnp.dot(h, w_ref[:, c * 512:(c + 1) * 512], preferred_element_type=F32) for c in range(QKV_DIM // 512)]
    heads = [ys[hd // 4][:, (hd % 4) * HEAD_DIM:(hd % 4 + 1) * HEAD_DIM] for hd in range(n_heads)]
    ms = [jnp.mean(heads[hd] * heads[hd], axis=-1, keepdims=True) for hd in range(n_rot)]
    inv = [lax.rsqrt(m + NORM_EPS) for m in ms]
    yn = [heads[hd] * inv[hd] * (qn_ref[...] if hd < Q_HEADS else kn_ref[...]) for hd in range(n_rot)]
    rot = [pltpu.roll(y, HEAD_DIM // 2, axis=1) for y in yn]
    for hd in range(n_heads):
        if hd < n_rot:
            yh = yn[hd] * cosf + rot[hd] * sinf
            if hd < Q_HEADS:
                yh = yh * q_scale
        else:
            yh = heads[hd]
        o_ref[:, hd * HEAD_DIM:(hd + 1) * HEAD_DIM] = yh.astype(o_ref.dtype)


def _qkv_proj(x, nw, w, layer, qn, kn, cosf, sinf, tm, seq, q_scale):
    t, d = x.shape
    tiles_per_seq = seq // tm
    return pl.pallas_call(
        functools.partial(_qkv_proj_kernel, q_scale=q_scale),
        grid=(t // tm,),
        in_specs=[
            pl.BlockSpec((tm, d), lambda i: (i, 0)),
            pl.BlockSpec((1, d), lambda i: (0, 0)),
            pl.BlockSpec((None, d, QKV_DIM), lambda i: (layer, 0, 0), pipeline_mode=pl.Buffered(1)),
            pl.BlockSpec((1, HEAD_DIM), lambda i: (0, 0)),
            pl.BlockSpec((1, HEAD_DIM), lambda i: (0, 0)),
            pl.BlockSpec((tm, HEAD_DIM), lambda i: (i % tiles_per_seq, 0)),
            pl.BlockSpec((tm, HEAD_DIM), lambda i: (i % tiles_per_seq, 0)),
        ],
        out_specs=pl.BlockSpec((tm, QKV_DIM), lambda i: (i, 0)),
        out_shape=jax.ShapeDtypeStruct((t, QKV_DIM), BF16),
        compiler_params=_cparams(("parallel",)),
        name="attn_qkv_proj",
    )(x, nw, w, qn, kn, cosf, sinf)


def _log_sigmoid(x):
    return jnp.minimum(x, 0.0) - jnp.log(1.0 + jnp.exp(-jnp.abs(x)))


def _silu(x):
    return x / (1.0 + jnp.exp(-x))


def _gla_kernel(q_ref, k_ref, v_ref, g_ref, r_ref, wg_ref, bg_ref, nw_ref, o_ref,
                la_ref, oacc_ref, sf_ref, sb_ref, *, n_chunks):
    c_len = GLA_CHUNK
    n_sub = GLA_GROUP
    g_len = n_sub * c_len
    n_groups = n_chunks // n_sub
    n_par = GLA_GROUPS_PER_STEP
    logits = jnp.dot(r_ref[...], wg_ref[0], preferred_element_type=F32) + bg_ref[0]
    la_ref[...] = _log_sigmoid(logits) * (1.0 / GLA_GATE_NORMALIZER)
    sf_ref[...] = jnp.zeros_like(sf_ref)
    sb_ref[...] = jnp.zeros_like(sb_ref)

    row = lax.broadcasted_iota(jnp.int32, (g_len, g_len), 0)
    col = lax.broadcasted_iota(jnp.int32, (g_len, g_len), 1)
    same_chunk = (row // c_len) == (col // c_len)
    cum_prefix = jnp.where(same_chunk & (row >= col), 1.0, 0.0).astype(BF16)
    q_scale = GLA_DK ** -0.5
    nw = nw_ref[...]
    trans_b = (((1,), (1,)), ((), ()))
    trans_a = (((0,), (0,)), ((), ()))

    def step(i, final):
        work = []
        for forward in (True, False):
            for s in range(n_par):
                gi = i * n_par + s
                g = gi if forward else n_groups - 1 - gi
                base = pl.multiple_of(g * g_len, g_len)
                if forward:
                    lanes, keep = slice(0, GLA_DK), row >= col
                else:
                    lanes, keep = slice(GLA_DK, 2 * GLA_DK), col > row
                work.append(dict(forward=forward, base=base, rows=pl.ds(base, g_len), lanes=lanes,
                                 mask=same_chunk & keep))

        n_half = len(work) // 2
        for fw, bw in zip(work[:n_half], work[n_half:]):
            la = jnp.concatenate([la_ref[fw["rows"], fw["lanes"]], la_ref[bw["rows"], bw["lanes"]]], axis=1)
            la_hi = la.astype(BF16)
            la_lo = (la - la_hi.astype(F32)).astype(BF16)
            pre = (jnp.dot(cum_prefix, la_hi, preferred_element_type=F32)
                   + jnp.dot(cum_prefix, la_lo, preferred_element_type=F32))
            fw["b"] = pre[:, :GLA_DK]
            pre_b = pre[:, GLA_DK:]
            tot = jnp.concatenate(
                [jnp.broadcast_to(pre_b[u * c_len + c_len - 1:u * c_len + c_len, :], (c_len, GLA_DK))
                 for u in range(n_sub)], axis=0)
            bw["b"] = tot - pre_b + la[:, GLA_DK:]

        for d in work:
            b = d["b"]
            ends = []
            for u in range(n_sub):
                r = u * c_len + (c_len - 1 if d["forward"] else 0)
                ends.append(b[r:r + 1, :])
            d["decay"] = [jnp.exp(e) for e in ends]
            dec_rows = jnp.concatenate([jnp.broadcast_to(e, (c_len, GLA_DK)) for e in d["decay"]], axis=0)
            q = q_ref[d["rows"], :].astype(F32)
            k = k_ref[d["rows"], :].astype(F32)
            d["v"] = v_ref[d["rows"], :]
            d["q_dec"] = (q * (jnp.exp(b) * q_scale)).astype(BF16)
            k_inv = k * jnp.exp(-b)
            d["k_out"] = (k_inv * dec_rows).astype(BF16)
            att = lax.dot_general(d["q_dec"], k_inv.astype(BF16), trans_b, preferred_element_type=F32)
            d["att"] = jnp.where(d["mask"], att, 0.0).astype(BF16)

        for d in work:
            d["o"] = jnp.dot(d["att"], d["v"], preferred_element_type=F32)
            d["kv"] = []
            for u in range(n_sub):
                sl = slice(u * c_len, (u + 1) * c_len)
                d["kv"].append(lax.dot_general(d["k_out"][sl, :], d["v"][sl, :], trans_a,
                                               preferred_element_type=F32))

        for forward in (True, False):
            st_ref = sf_ref if forward else sb_ref
            state = st_ref[...]
            for d in (w for w in work if w["forward"] == forward):
                for u in (range(n_sub) if forward else range(n_sub - 1, -1, -1)):
                    sl = slice(u * c_len, (u + 1) * c_len)
                    o_u = d["o"][sl, :] + jnp.dot(d["q_dec"][sl, :], state.astype(BF16),
                                                  preferred_element_type=F32)
                    dec_col = jnp.broadcast_to(d["decay"][u], (GLA_DK, GLA_DK)).T
                    state = state * jnp.concatenate([dec_col] * (GLA_DV // GLA_DK), axis=1) + d["kv"][u]
                    rows_u = pl.ds(pl.multiple_of(d["base"] + u * c_len, c_len), c_len)
                    if final:
                        o_sum = o_u + oacc_ref[rows_u, :]
                        g = g_ref[rows_u, :].astype(F32)
                        o_ref[rows_u, :] = (_rmsnorm_rows(o_sum, nw) * _silu(g)).astype(o_ref.dtype)
                    else:
                        oacc_ref[rows_u, :] = o_u
            st_ref[...] = state

    n_steps = n_groups // n_par
    half = n_steps // 2

    def first_half(i, carry):
        step(i, False)
        return carry

    def second_half(i, carry):
        step(i, True)
        return carry

    lax.fori_loop(0, half, first_half, 0)
    lax.fori_loop(half, n_steps, second_half, 0)


def _gla(proj, wg, bg, nw, batch, seq):
    t = batch * seq
    n_chunks = seq // GLA_CHUNK
    assert n_chunks % (2 * GLA_GROUP * GLA_GROUPS_PER_STEP) == 0
    kq = GLA_KEY_DIM // GLA_DK
    return pl.pallas_call(
        functools.partial(_gla_kernel, n_chunks=n_chunks),
        grid=(batch, GLA_HEADS),
        in_specs=[
            pl.BlockSpec((seq, GLA_DK), lambda b, h: (b, h)),
            pl.BlockSpec((seq, GLA_DK), lambda b, h: (b, kq + h)),
            pl.BlockSpec((seq, GLA_DV), lambda b, h: (b, (2 * GLA_KEY_DIM) // GLA_DV + h)),
            pl.BlockSpec((seq, GLA_DV), lambda b, h: (b, (2 * GLA_KEY_DIM + GLA_VAL_DIM) // GLA_DV + h)),
            pl.BlockSpec((seq, 128), lambda b, h: (b, GLA_MAIN_COLS // 128)),
            pl.BlockSpec((1, 128, 2 * GLA_DK), lambda b, h: (h, 0, 0)),
            pl.BlockSpec((1, 1, 2 * GLA_DK), lambda b, h: (h, 0, 0)),
            pl.BlockSpec((1, GLA_DV), lambda b, h: (0, 0)),
        ],
        out_specs=pl.BlockSpec((seq, GLA_DV), lambda b, h: (b, h)),
        out_shape=jax.ShapeDtypeStruct((t, GLA_VAL_DIM), BF16),
        scratch_shapes=[
            pltpu.VMEM((seq, 2 * GLA_DK), F32),
            pltpu.VMEM((seq, GLA_DV), F32),
            pltpu.VMEM((GLA_DK, GLA_DV), F32),
            pltpu.VMEM((GLA_DK, GLA_DV), F32),
        ],
        compiler_params=_cparams(("parallel", "parallel")),
        name="gla_scan",
    )(proj, proj, proj, proj, proj, wg, bg, nw)


def _attn_kernel(q_ref, k_ref, v_ref, o_ref, s_ref, m_ref, acc_ref, va_ref, *, tq, tk, n_q, n_kv):
    assert n_kv % 2 == 0

    def scores(tile, blk, slot):
        rows = pl.ds(pl.multiple_of(tile * tq, tq), tq)
        kb = k_ref[blk * tk:(blk + 1) * tk, :]
        for g in range(GROUP):
            qg = q_ref[rows, g * HEAD_DIM:(g + 1) * HEAD_DIM]
            s_ref[slot, g * tq:(g + 1) * tq, :] = lax.dot_general(
                qg, kb, (((1,), (1,)), ((), ())), preferred_element_type=F32)

    def softmax_pv(tile, blk, slot):
        vb = va_ref[blk * tk:(blk + 1) * tk, :]
        for g in range(GROUP):
            gr = slice(g * tq, (g + 1) * tq)
            s = s_ref[slot, gr, :]
            m_blk = jnp.max(s, axis=-1, keepdims=True)
            if blk == 0:
                m_new = m_blk
            else:
                m_old = m_ref[gr, :]
                m_new = jnp.maximum(m_old, m_blk)
                alpha = jnp.exp2(m_old - m_new)
            p = jnp.exp2(s - m_new).astype(BF16)
            pv = jnp.dot(p, vb, preferred_element_type=F32)
            acc_new = pv if blk == 0 else alpha * acc_ref[gr, :] + pv
            if blk == n_kv - 1:
                rows = pl.ds(pl.multiple_of(tile * tq, tq), tq)
                o_ref[rows, g * HEAD_DIM:(g + 1) * HEAD_DIM] = (
                    acc_new[:, :HEAD_DIM] / acc_new[:, HEAD_DIM:]).astype(o_ref.dtype)
            else:
                m_ref[gr, :] = m_new
                acc_ref[gr, :] = acc_new

    va_ref[:, :HEAD_DIM] = v_ref[...]
    va_ref[:, HEAD_DIM:] = jnp.ones((va_ref.shape[0], HEAD_DIM), BF16)
    scores(0, 0, 0)

    def tile_body(qi, carry):
        for j in range(n_kv):
            if j + 1 < n_kv:
                scores(qi, j + 1, (j + 1) % 2)
            else:
                scores(jnp.minimum(qi + 1, n_q - 1), 0, 0)
            softmax_pv(qi, j, j % 2)
        return carry

    lax.fori_loop(0, n_q, tile_body, 0)


def _attention(qkv, batch, seq, tq, tk):
    t = batch * seq
    gw = GROUP * HEAD_DIM
    rows = GROUP * tq
    return pl.pallas_call(
        functools.partial(_attn_kernel, tq=tq, tk=tk, n_q=seq // tq, n_kv=seq // tk),
        grid=(batch, KV_HEADS),
        in_specs=[
            pl.BlockSpec((seq, gw), lambda b, h: (b, h)),
            pl.BlockSpec((seq, HEAD_DIM), lambda b, h: (b, Q_HEADS + h)),
            pl.BlockSpec((seq, HEAD_DIM), lambda b, h: (b, Q_HEADS + KV_HEADS + h)),
        ],
        out_specs=pl.BlockSpec((seq, gw), lambda b, h: (b, h)),
        out_shape=jax.ShapeDtypeStruct((t, Q_HEADS * HEAD_DIM), BF16),
        scratch_shapes=[
            pltpu.VMEM((2, rows, tk), F32),
            pltpu.VMEM((rows, 1), F32),
            pltpu.VMEM((rows, 2 * HEAD_DIM), F32),
            pltpu.VMEM((seq, 2 * HEAD_DIM), BF16),
        ],
        compiler_params=_cparams(("parallel", "parallel")),
        name="gqa_attention",
    )(qkv, qkv, qkv)


def _ffn_kernel(x_ref, xp_ref, xn_ref, a_ref, ap_ref, an_ref, wo_ref, nw_ref, wu_ref, wc_ref, bc_ref, wd_ref,
                o_ref, h_ref, u_ref, act_ref, acc_ref, *, tm, tiles_per_seq):
    i = pl.program_id(0)
    nw = nw_ref[...]
    wo = wo_ref[...]

    def mixed(xr, ar):
        return xr[...] + jnp.dot(ar[...], wo, preferred_element_type=F32)

    x = mixed(x_ref, a_ref)
    keep_prev = jnp.where(i % tiles_per_seq == 0, 0.0, 1.0)
    keep_next = jnp.where(i % tiles_per_seq == tiles_per_seq - 1, 0.0, 1.0)
    h_ref[0:HALO, :] = (_rmsnorm_rows(mixed(xp_ref, ap_ref), nw) * keep_prev).astype(BF16)
    h_ref[HALO:HALO + tm, :] = _rmsnorm_rows(x, nw).astype(BF16)
    h_ref[HALO + tm:2 * HALO + tm, :] = (_rmsnorm_rows(mixed(xn_ref, an_ref), nw) * keep_next).astype(BF16)
    acc_ref[...] = x
    fc = FF_CHUNK
    n_chunks = D_FF // fc
    rb = FFN_ROW_BLOCK
    win = rb + 16

    ext = tm + 2 * HALO
    up_rows = [(0, 144), (144, 272), (272, 400), (400, ext)]
    down_rows = [(k * (tm // 4), (k + 1) * (tm // 4)) for k in range(4)]

    def up_piece(c, part, lo, hi):
        col0 = part * D_FF + c * fc
        u_ref[c % 2, part, lo:hi, :] = jnp.dot(h_ref[lo:hi, :], wu_ref[:, col0:col0 + fc],
                                               preferred_element_type=F32)

    def down_piece(c, lo, hi):
        acc_ref[lo:hi, :] += jnp.dot(act_ref[c % 2, lo:hi, :], wd_ref[c * fc:(c + 1) * fc, :],
                                     preferred_element_type=F32)

    def conv(slot, part, r, col0):
        uw = u_ref[slot, part, HALO - 8 + r * rb:HALO + 8 + (r + 1) * rb, :]
        wc = wc_ref[:, col0:col0 + fc]
        u_prev = pltpu.roll(uw, 1, axis=0)[8:8 + rb, :]
        u_next = pltpu.roll(uw, win - 1, axis=0)[8:8 + rb, :]
        return (u_prev * wc[0:1, :] + uw[8:8 + rb, :] * wc[1:2, :] + u_next * wc[2:3, :]
                + bc_ref[:, col0:col0 + fc])

    def elem_block(c, r):
        val = conv(c % 2, 0, r, c * fc)
        gate = conv(c % 2, 1, r, D_FF + c * fc)
        act_ref[c % 2, r * rb:(r + 1) * rb, :] = (_silu(gate) * val).astype(BF16)

    def stage(c):
        pieces = []
        if c + 1 < n_chunks:
            for lo, hi in up_rows:
                pieces.append(functools.partial(up_piece, c + 1, 0, lo, hi))
                pieces.append(functools.partial(up_piece, c + 1, 1, lo, hi))
        if c > 0:
            for lo, hi in down_rows:
                pieces.append(functools.partial(down_piece, c - 1, lo, hi))
        n_blocks = tm // rb
        done = 0
        for k, piece in enumerate(pieces):
            piece()
            upto = (k + 1) * n_blocks // len(pieces)
            for r in range(done, upto):
                elem_block(c, r)
            done = upto

    for lo, hi in up_rows:
        up_piece(0, 0, lo, hi)
        up_piece(0, 1, lo, hi)
    for c in range(n_chunks):
        stage(c)
    for lo, hi in down_rows:
        down_piece(n_chunks - 1, lo, hi)
    o_ref[...] = acc_ref[...]


def _mix_ffn(x, a, wo, nw, wu, wc, bc, wd, layer, mixer, tm, seq):
    t, d = x.shape
    ka = a.shape[1]
    tiles_per_seq = seq // tm
    hb = tm // HALO
    n_hblk = t // HALO
    main = lambda i: (i, 0)
    prev = lambda i: (jnp.maximum(i * hb - 1, 0), 0)
    nxt = lambda i: (jnp.minimum((i + 1) * hb, n_hblk - 1), 0)
    const = lambda i: (0, 0)
    return pl.pallas_call(
        functools.partial(_ffn_kernel, tm=tm, tiles_per_seq=tiles_per_seq),
        grid=(t // tm,),
        in_specs=[
            pl.BlockSpec((tm, d), main),
            pl.BlockSpec((HALO, d), prev),
            pl.BlockSpec((HALO, d), nxt),
            pl.BlockSpec((tm, ka), main),
            pl.BlockSpec((HALO, ka), prev),
            pl.BlockSpec((HALO, ka), nxt),
            pl.BlockSpec((None, ka, d), lambda i: (mixer, 0, 0), pipeline_mode=pl.Buffered(1)),
            pl.BlockSpec((1, d), const),
            pl.BlockSpec((None, d, 2 * D_FF), lambda i: (layer, 0, 0), pipeline_mode=pl.Buffered(1)),
            pl.BlockSpec((None, 3, 2 * D_FF), lambda i: (layer, 0, 0)),
            pl.BlockSpec((None, 1, 2 * D_FF), lambda i: (layer, 0, 0)),
            pl.BlockSpec((None, D_FF, d), lambda i: (layer, 0, 0), pipeline_mode=pl.Buffered(1)),
        ],
        out_specs=pl.BlockSpec((tm, d), main),
        out_shape=jax.ShapeDtypeStruct((t, d), F32),
        scratch_shapes=[
            pltpu.VMEM((tm + 2 * HALO, d), BF16),
            pltpu.VMEM((2, 2, tm + 2 * HALO, FF_CHUNK), F32),
            pltpu.VMEM((2, tm, FF_CHUNK), BF16),
            pltpu.VMEM((tm, d), F32),
        ],
        compiler_params=_cparams(("parallel",)),
        name="mix_conv_ffn",
    )(x, x, x, a, a, a, wo, nw, wu, wc, bc, wd)


def _gla_gate_weights(w_f, b_f, w_b, b_b):
    wf = w_f.reshape(GLA_RANK, GLA_HEADS, GLA_DK).transpose(1, 0, 2)
    wb = w_b.reshape(GLA_RANK, GLA_HEADS, GLA_DK).transpose(1, 0, 2)
    z = jnp.zeros_like(wf)
    top = jnp.concatenate([wf, z], axis=-1)
    mid = jnp.concatenate([z, wb], axis=-1)
    pad = jnp.zeros((GLA_HEADS, 128 - 2 * GLA_RANK, 2 * GLA_DK), w_f.dtype)
    wg = jnp.concatenate([top, mid, pad], axis=1).astype(BF16)
    bg = jnp.concatenate([b_f.reshape(GLA_HEADS, 1, GLA_DK), b_b.reshape(GLA_HEADS, 1, GLA_DK)], axis=-1)
    return wg, bg.astype(F32)


def _rope_tables(seq):
    rows = seq // GRID_W
    row_idx = jnp.repeat(jnp.arange(rows, dtype=F32), GRID_W)
    col_idx = jnp.tile(jnp.arange(GRID_W, dtype=F32), rows)
    inv_freq = ROPE_THETA ** (-jnp.arange(ROPE_PAIRS_PER_AXIS, dtype=F32) / ROPE_PAIRS_PER_AXIS)
    ang = jnp.concatenate([row_idx[:, None] * inv_freq, col_idx[:, None] * inv_freq], axis=-1)
    cos = jnp.cos(ang)
    sin = jnp.sin(ang)
    return jnp.concatenate([cos, cos], axis=-1), jnp.concatenate([-sin, sin], axis=-1)


def kernel(x, norm_mix, norm_ffn, gla_w_in, gla_w_gate_up_f, gla_b_gate_f, gla_w_gate_up_b, gla_b_gate_b,
           gla_norm, gla_w_out, attn_w_qkv, attn_q_norm, attn_k_norm, attn_w_out,
           ffn_w_up, ffn_w_conv, ffn_b_conv, ffn_w_down):
    batch, seq, d = x.shape
    depth = norm_mix.shape[0]
    t = batch * seq
    tm = 512
    cosf, sinf = _rope_tables(seq)
    q_scale = (HEAD_DIM ** -0.5) * math.log2(math.e)
    xf = x.reshape(t, d)
    w_in_all = jnp.pad(gla_w_in, ((0, 0), (0, 0), (0, GLA_PROJ_COLS - gla_w_in.shape[-1]))).astype(BF16)
    w_qkv_all = attn_w_qkv.astype(BF16)
    gla_wo_all = gla_w_out.astype(BF16)
    attn_wo_all = attn_w_out.astype(BF16)
    wu_all = ffn_w_up.astype(BF16)
    wd_all = ffn_w_down.astype(BF16)
    bc_all = ffn_b_conv.reshape(depth, 1, 2 * D_FF)
    for i in range(depth):
        j = i // 2
        nw = norm_mix[i].reshape(1, d)
        if i % 2 == 0:
            wg, bg = _gla_gate_weights(gla_w_gate_up_f[j], gla_b_gate_f[j], gla_w_gate_up_b[j], gla_b_gate_b[j])
            proj = _norm_proj(xf, nw, w_in_all, j, tm)
            a = _gla(proj, wg, bg, gla_norm[j].reshape(1, GLA_DV), batch, seq)
            wo_all = gla_wo_all
        else:
            qkv = _qkv_proj(xf, nw, w_qkv_all, j, attn_q_norm[j].reshape(1, HEAD_DIM),
                            attn_k_norm[j].reshape(1, HEAD_DIM), cosf, sinf, tm, seq, q_scale)
            a = _attention(qkv, batch, seq, tq=256, tk=min(2048, seq // 2))
            wo_all = attn_wo_all
        xf = _mix_ffn(xf, a, wo_all, norm_ffn[i].reshape(1, d), wu_all, ffn_w_conv, bc_all, wd_all,
                      i, j, tm, seq)
    return xf.reshape(batch, seq, d)
```
